```python
import math
import jax, jax.numpy as jnp
from jax import lax
import numpy as np

D_MODEL = 1024
BATCH = 4
SEQ = 8192
DEPTH = 1
DEC_BATCH = 32
DEC_SEQ = 4
PAST_LEN = 16384
PAGE_SIZE = 128

DN_HEADS = 4
DN_DK = 128
DN_DV = 128
CONV_W = 4
DN_CHUNK = 64
DN_QK = DN_HEADS * DN_DK
DN_V = DN_HEADS * DN_DV
CONV_DIM = 2 * DN_QK + DN_V
SW_GROUPS = ((128, 1), (512, 4), (2048, 16))
SW_HEADS = 4
SW_DH = 64
SW_W = SW_HEADS * SW_DH
SW_BLOCK = 128
IN_SIZES = (DN_QK, DN_QK, DN_V, DN_HEADS, DN_HEADS, DN_V) + (SW_W,) * (3 * len(SW_GROUPS))
IN_DIM = sum(IN_SIZES)
MIX_OUT = DN_V + SW_W
MEM_LEN = 256
MEM_HEADS = 4
MEM_DH = D_MODEL // MEM_HEADS
PEER_HEADS = 8
PEER_NKEYS = 128
PEER_N = PEER_NKEYS * PEER_NKEYS
PEER_DQ = 256
PEER_DHALF = PEER_DQ // 2
PEER_TOPK = 16
PEER_BLOCK = 128
NORM_EPS = 1e-6

kernel_name = 'hymba_deltanet_dilated_peer_step'

F32 = jnp.float32


def rmsnorm(x, g):
    xf = x.astype(F32)
    y = xf * lax.rsqrt(jnp.mean(xf * xf, axis=-1, keepdims=True) + NORM_EPS)
    return (y * g.astype(F32)).astype(x.dtype)


def l2norm(x):
    xf = x.astype(F32)
    return xf * lax.rsqrt(jnp.sum(xf * xf, axis=-1, keepdims=True) + NORM_EPS)


def short_conv(x, buf, w):
    t = x.shape[1]
    xp = jnp.concatenate([buf.astype(x.dtype), x], axis=1)
    y = xp[:, 0:t] * w[0]
    for j in range(1, CONV_W):
        y = y + xp[:, j:j + t] * w[j]
    return jax.nn.silu(y), xp[:, t:]


def gated_delta_rule(q, k, v, g, beta, s0):
    bn, t, nh, dk = q.shape
    dv = v.shape[-1]
    c = DN_CHUNK
    pad = (-t) % c
    nc = (t + pad) // c

    def prep(a):
        a = jnp.pad(a, [(0, 0), (0, pad)] + [(0, 0)] * (a.ndim - 2))
        a = a.reshape((bn, nc, c) + a.shape[2:])
        return jnp.moveaxis(a, 3, 2)

    q, k, v, g, beta = prep(q), prep(k), prep(v), prep(g), prep(beta)
    gc = jnp.cumsum(g, axis=-1)
    idx = jnp.arange(c)
    causal = idx[:, None] >= idx[None, :]
    strict = idx[:, None] > idx[None, :]
    gam = jnp.exp(jnp.where(causal, gc[..., :, None] - gc[..., None, :], -jnp.inf))
    kb = k * beta[..., None]
    lmat = jnp.where(strict, jnp.einsum('bnhik,bnhjk->bnhij', kb, k) * gam, 0.0)
    eye = jnp.broadcast_to(jnp.eye(c, dtype=F32), lmat.shape)
    rhs = jnp.concatenate([v * beta[..., None], kb * jnp.exp(gc)[..., None]], axis=-1)
    sol = lax.linalg.triangular_solve(eye + lmat, rhs, left_side=True, lower=True, unit_diagonal=True)
    u0, wk = sol[..., :dv], sol[..., dv:]
    aqk = jnp.where(causal, jnp.einsum('bnhik,bnhjk->bnhij', q, k) * gam, 0.0)
    qg = q * jnp.exp(gc)[..., None]
    kd = k * jnp.exp(gc[..., -1:] - gc)[..., None]
    dlast = jnp.exp(gc[..., -1])

    def step(s, xs):
        u0_c, w_c, aqk_c, qg_c, kd_c, dl_c = xs
        u = u0_c - jnp.einsum('bhck,bhkv->bhcv', w_c, s)
        o = jnp.einsum('bhck,bhkv->bhcv', qg_c, s) + jnp.einsum('bhij,bhjv->bhiv', aqk_c, u)
        s = s * dl_c[..., None, None] + jnp.einsum('bhck,bhcv->bhkv', kd_c, u)
        return s, o

    xs = (jnp.moveaxis(u0, 1, 0), jnp.moveaxis(wk, 1, 0), jnp.moveaxis(aqk, 1, 0),
          jnp.moveaxis(qg, 1, 0), jnp.moveaxis(kd, 1, 0), jnp.moveaxis(dlast, 1, 0))
    s_fin, o = lax.scan(step, s0, xs)
    o = jnp.moveaxis(o, 0, 1)
    o = jnp.moveaxis(o, 2, 3).reshape(bn, nc * c, nh, dv)[:, :t]
    return o, s_fin


def deltanet_group(qa, ka, va, ag, bg, z, conv_buf, s0, conv_w, a_log, dt_bias, g_onorm):
    bn, t, _ = qa.shape
    act, conv_new = short_conv(jnp.concatenate([qa, ka, va], axis=-1), conv_buf, conv_w)
    q, k, v = jnp.split(act, [DN_QK, 2 * DN_QK], axis=-1)
    q = l2norm(q.reshape(bn, t, DN_HEADS, DN_DK)) * (DN_DK ** -0.5)
    k = l2norm(k.reshape(bn, t, DN_HEADS, DN_DK))
    v = v.reshape(bn, t, DN_HEADS, DN_DV).astype(F32)
    beta = jax.nn.sigmoid(bg.astype(F32))
    g = -jnp.exp(a_log.astype(F32)) * jax.nn.softplus(ag.astype(F32) + dt_bias.astype(F32))
    o, s_new = gated_delta_rule(q, k, v, g, beta, s0.astype(F32))
    o = rmsnorm(o, g_onorm) * jax.nn.silu(z.reshape(bn, t, DN_HEADS, DN_DV).astype(F32))
    return o.reshape(bn, t, DN_V).astype(qa.dtype), conv_new, s_new.astype(qa.dtype)


def dilated_prompt(q, k, v, window, dil):
    bn, t, nh, dh = q.shape
    ln = t // dil
    nb = -(-ln // SW_BLOCK)
    lp = nb * SW_BLOCK
    span = window // dil

    def sub(a):
        a = a.reshape(bn, ln, dil, nh, dh).transpose(0, 2, 1, 3, 4)
        a = jnp.pad(a, ((0, 0), (0, 0), (0, lp - ln), (0, 0), (0, 0)))
        return a.reshape(bn, dil, nb, SW_BLOCK, nh, dh)

    def band(a):
        prev = jnp.pad(a, ((0, 0), (0, 0), (1, 0), (0, 0), (0, 0), (0, 0)))[:, :, :-1]
        return jnp.concatenate([prev, a], axis=3)

    qs = sub(q)
    kk, vv = band(sub(k)), band(sub(v))
    s = jnp.einsum('brnqhc,brnkhc->brnhqk', qs, kk, preferred_element_type=F32) * (dh ** -0.5)
    qi = jnp.arange(SW_BLOCK)[:, None]
    ki = jnp.arange(2 * SW_BLOCK)[None, :]
    dist = SW_BLOCK + qi - ki
    kpos = (jnp.arange(nb)[:, None, None] - 1) * SW_BLOCK + ki[None]
    mask = (dist >= 0) & (dist <= span) & (kpos >= 0)
    s = jnp.where(mask[None, None, :, None], s, -jnp.inf)
    lse = jax.nn.logsumexp(s, axis=-1)
    p = jnp.exp(s - lse[..., None])
    o = jnp.einsum('brnhqk,brnkhc->brnqhc', p, vv.astype(F32))
    o = o.reshape(bn, dil, lp, nh, dh)[:, :, :ln].transpose(0, 2, 1, 3, 4).reshape(bn, t, nh, dh)
    lse = lse.transpose(0, 1, 2, 4, 3).reshape(bn, dil, lp, nh)[:, :, :ln]
    lse = lse.transpose(0, 2, 1, 3).reshape(bn, t, nh)
    return o, lse


def dilated_sample(q, k, v, buf, window, dil):
    bn, tn, nh, dh = q.shape
    wb = buf.shape[1]
    kc = jnp.concatenate([buf[:, :, 0].astype(k.dtype), k], axis=1)
    vc = jnp.concatenate([buf[:, :, 1].astype(v.dtype), v], axis=1)
    m = jnp.arange(window // dil + 1)
    j = wb + jnp.arange(tn)[:, None] - m[None, :] * dil
    valid = j >= 0
    jc = jnp.clip(j, 0)
    kg = kc[:, jc]
    vg = vc[:, jc]
    s = jnp.einsum('bthc,btmhc->bhtm', q, kg, preferred_element_type=F32) * (dh ** -0.5)
    s = jnp.where(valid[None, None], s, -jnp.inf)
    lse = jax.nn.logsumexp(s, axis=-1)
    p = jnp.exp(s - lse[..., None])
    o = jnp.einsum('bhtm,btmhc->bthc', p, vg.astype(F32))
    new_buf = jnp.stack([kc, vc], axis=2)[:, -min(window, wb + tn):]
    return o, lse.transpose(0, 2, 1), new_buf


def token_mixer(a, conv_buf, s0, wins, w_in, conv_w, a_log, dt_bias, g_onorm, w_out):
    bn, t, _ = a.shape
    proj = a @ w_in
    parts = jnp.split(proj, [int(o) for o in np.cumsum(IN_SIZES)[:-1]], axis=-1)
    qa, ka, va, ag, bg, z = parts[:6]
    sw = parts[6:]
    if conv_buf is None:
        conv_buf = jnp.zeros((bn, CONV_W - 1, CONV_DIM), a.dtype)
    if s0 is None:
        s0 = jnp.zeros((bn, DN_HEADS, DN_DK, DN_DV), F32)
    o_dn, conv_new, s_new = deltanet_group(qa, ka, va, ag, bg, z, conv_buf, s0,
                                           conv_w, a_log, dt_bias, g_onorm)
    outs, lses, wins_new = [], [], []
    for gi, (win, dil) in enumerate(SW_GROUPS):
        q, k, v = [p.reshape(bn, t, SW_HEADS, SW_DH) for p in sw[3 * gi:3 * gi + 3]]
        if wins is None:
            o, l = dilated_prompt(q, k, v, win, dil)
            nbuf = jnp.stack([k, v], axis=2)[:, -min(win, t):]
        else:
            o, l, nbuf = dilated_sample(q, k, v, wins[gi], win, dil)
        outs.append(o)
        lses.append(l)
        wins_new.append(nbuf)
    wgt = jax.nn.softmax(jnp.stack(lses, axis=0), axis=0)
    o_sw = jnp.sum(wgt[..., None] * jnp.stack(outs, axis=0), axis=0).reshape(bn, t, SW_W)
    y = jnp.concatenate([o_dn, o_sw.astype(a.dtype)], axis=-1) @ w_out
    return y, conv_new, s_new, wins_new


def memory_kv(mem, g_memkv, w_mkv):
    bn = mem.shape[0]
    return (rmsnorm(mem, g_memkv) @ w_mkv).reshape(bn, MEM_LEN, 2, MEM_HEADS, MEM_DH)


def mem_attend(c, kv, w_mq, w_mo):
    bn, t, _ = c.shape
    q = (c @ w_mq).reshape(bn, t, MEM_HEADS, MEM_DH)
    s = jnp.einsum('bthc,bmhc->bhtm', q, kv[:, :, 0], preferred_element_type=F32) * (MEM_DH ** -0.5)
    p = jax.nn.softmax(s, axis=-1)
    o = jnp.einsum('bhtm,bmhc->bthc', p, kv[:, :, 1].astype(F32))
    return o.reshape(bn, t, MEM_HEADS * MEM_DH).astype(c.dtype) @ w_mo


def peer(f, w_pq, sub_keys, expert_u, expert_v):
    bn, t, d = f.shape
    n = bn * t
    pad = (-n) % PEER_BLOCK
    flat = jnp.pad(f.reshape(n, d), ((0, pad), (0, 0))).reshape(-1, PEER_BLOCK, d)

    def block(xb):
        q = (xb @ w_pq).reshape(PEER_BLOCK, PEER_HEADS, 2, PEER_DHALF)
        s = jnp.einsum('thpc,hpnc->thpn', q, sub_keys, preferred_element_type=F32)
        sv, si = lax.top_k(s, PEER_TOPK)
        cand = sv[:, :, 0, :, None] + sv[:, :, 1, None, :]
        cv, ci = lax.top_k(cand.reshape(PEER_BLOCK, PEER_HEADS, PEER_TOPK * PEER_TOPK), PEER_TOPK)
        i1 = jnp.take_along_axis(si[:, :, 0], ci // PEER_TOPK, axis=-1)
        i2 = jnp.take_along_axis(si[:, :, 1], ci % PEER_TOPK, axis=-1)
        eid = i1 * PEER_NKEYS + i2
        gate = jax.nn.softmax(cv, axis=-1)
        hid = jax.nn.gelu(jnp.einsum('thkd,td->thk', expert_u[eid], xb, preferred_element_type=F32))
        out = jnp.einsum('thk,thkd->td', gate * hid, expert_v[eid].astype(F32))
        return out.astype(f.dtype)

    out = lax.map(block, flat)
    return out.reshape(-1, d)[:n].reshape(bn, t, d)


def layer(h, conv_buf, s0, wins, kv, g_mix, w_in, conv_w, a_log, dt_bias, g_onorm, w_out,
          g_memq, w_mq, w_mo, g_ffn, w_pq, sub_keys, expert_u, expert_v):
    mix, conv_new, s_new, wins_new = token_mixer(rmsnorm(h, g_mix), conv_buf, s0, wins, w_in,
                                                 conv_w, a_log, dt_bias, g_onorm, w_out)
    h = h + mix
    h = h + mem_attend(rmsnorm(h, g_memq), kv, w_mq, w_mo)
    h = h + peer(rmsnorm(h, g_ffn), w_pq, sub_keys, expert_u, expert_v)
    return h, conv_new, s_new, wins_new


def setup_inputs(seed: int = 0) -> dict:
    key = jax.random.key(seed)
    ks = iter(jax.random.split(key, 40))

    def nrm(shape, scale):
        return jax.random.normal(next(ks), shape, F32) * scale

    def gain(shape):
        return 1.0 + nrm(shape, 0.02)

    wb = [min(w, PAST_LEN) for w, _ in SW_GROUPS]
    a_log = jnp.log(jax.random.uniform(next(ks), (DEPTH, DN_HEADS), F32, 1.0, 16.0))
    dt = jnp.exp(jax.random.uniform(next(ks), (DEPTH, DN_HEADS), F32, math.log(1e-3), math.log(1e-1)))
    dt_bias = dt + jnp.log(-jnp.expm1(-dt))
    return {
        'x_prompt': nrm((BATCH, SEQ, D_MODEL), 1.0),
        'x_sample': nrm((DEC_BATCH, DEC_SEQ, D_MODEL), 1.0),
        'state_delta': nrm((DEPTH, DEC_BATCH, DN_HEADS, DN_DK, DN_DV), 0.1),
        'state_conv': nrm((DEPTH, DEC_BATCH, CONV_W - 1, CONV_DIM), 1.0),
        'cache_win1': nrm((DEPTH, DEC_BATCH, wb[0], 2, SW_HEADS, SW_DH), 1.0),
        'cache_win2': nrm((DEPTH, DEC_BATCH, wb[1], 2, SW_HEADS, SW_DH), 1.0),
        'cache_win3': nrm((DEPTH, DEC_BATCH, wb[2], 2, SW_HEADS, SW_DH), 1.0),
        'cache_mem_kv': nrm((DEPTH, DEC_BATCH, MEM_LEN, 2, MEM_HEADS, MEM_DH), 1.0),
        'mem_prompt': nrm((BATCH, MEM_LEN, D_MODEL), 1.0),
        'g_mix': gain((DEPTH, D_MODEL)),
        'w_in': nrm((DEPTH, D_MODEL, IN_DIM), D_MODEL ** -0.5),
        'conv_w': nrm((DEPTH, CONV_W, CONV_DIM), CONV_W ** -0.5),
        'a_log': a_log,
        'dt_bias': dt_bias,
        'g_onorm': gain((DEPTH, DN_DV)),
        'w_out': nrm((DEPTH, MIX_OUT, D_MODEL), MIX_OUT ** -0.5),
        'g_memq': gain((DEPTH, D_MODEL)),
        'g_memkv': gain((DEPTH, D_MODEL)),
        'w_mq': nrm((DEPTH, D_MODEL, MEM_HEADS * MEM_DH), D_MODEL ** -0.5),
        'w_mkv': nrm((DEPTH, D_MODEL, 2 * MEM_HEADS * MEM_DH), D_MODEL ** -0.5),
        'w_mo': nrm((DEPTH, MEM_HEADS * MEM_DH, D_MODEL), (MEM_HEADS * MEM_DH) ** -0.5),
        'g_ffn': gain((DEPTH, D_MODEL)),
        'w_pq': nrm((DEPTH, D_MODEL, PEER_HEADS * PEER_DQ), D_MODEL ** -0.5),
        'sub_keys': nrm((DEPTH, PEER_HEADS, 2, PEER_NKEYS, PEER_DHALF), PEER_DHALF ** -0.5),
        'expert_u': nrm((DEPTH, PEER_N, D_MODEL), D_MODEL ** -0.5),
        'expert_v': nrm((DEPTH, PEER_N, D_MODEL), 0.5 * PEER_HEADS ** -0.5),
        'g_final': gain((D_MODEL,)),
    }


def reference(x_prompt, x_sample, state_delta, state_conv, cache_win1, cache_win2, cache_win3,
              cache_mem_kv, mem_prompt, g_mix, w_in, conv_w, a_log, dt_bias, g_onorm, w_out,
              g_memq, g_memkv, w_mq, w_mkv, w_mo, g_ffn, w_pq, sub_keys, expert_u, expert_v,
              g_final):
    per_layer = (g_mix, w_in, conv_w, a_log, dt_bias, g_onorm, w_out, g_memq, w_mq, w_mo,
                 g_ffn, w_pq, sub_keys, expert_u, expert_v)
    h = x_prompt
    p_conv, p_delta, p_win, p_mem = [], [], [], []
    for l in range(DEPTH):
        kv = memory_kv(mem_prompt, g_memkv[l], w_mkv[l])
        h, c_new, s_new, w_new = layer(h, None, None, None, kv, *[p[l] for p in per_layer])
        p_conv.append(c_new)
        p_delta.append(s_new)
        p_win.append(w_new)
        p_mem.append(kv)
    y_prompt = rmsnorm(h, g_final)
    h = x_sample
    s_conv, s_delta, s_win = [], [], []
    for l in range(DEPTH):
        wins = (cache_win1[l], cache_win2[l], cache_win3[l])
        h, c_new, s_new, w_new = layer(h, state_conv[l], state_delta[l], wins, cache_mem_kv[l],
                                       *[p[l] for p in per_layer])
        s_conv.append(c_new)
        s_delta.append(s_new)
        s_win.append(w_new)
    y_sample = rmsnorm(h, g_final)
    return (y_prompt, y_sample,
            jnp.stack(p_delta), jnp.stack(p_conv),
            jnp.stack([w[0] for w in p_win]), jnp.stack([w[1] for w in p_win]),
            jnp.stack([w[2] for w in p_win]), jnp.stack(p_mem),
            jnp.stack(s_delta), jnp.stack(s_conv),
            jnp.stack([w[0] for w in s_win]), jnp.stack([w[1] for w in s_win]),
            jnp.stack([w[2] for w in s_win]))
```

```python
import functools
import math

import jax
import jax.numpy as jnp
from jax import lax
from jax.experimental import pallas as pl
from jax.experimental.pallas import tpu as pltpu

F32 = jnp.float32
BF16 = jnp.bfloat16
HIGHEST = lax.Precision.HIGHEST

D_MODEL = 1024
DN_HEADS = 4
DN_DK = 128
DN_DV = 128
CONV_W = 4
DN_QK = DN_HEADS * DN_DK
DN_V = DN_HEADS * DN_DV
CONV_DIM = 2 * DN_QK + DN_V
SW_GROUPS = ((128, 1), (512, 4), (2048, 16))
SW_HEADS = 4
SW_DH = 64
SW_W = SW_HEADS * SW_DH
SW_BLOCK = 128
SW_ALL = 3 * len(SW_GROUPS) * SW_W
MEM_LEN = 256
MEM_HEADS = 4
MEM_DH = D_MODEL // MEM_HEADS
PEER_HEADS = 8
PEER_NKEYS = 128
PEER_N = PEER_NKEYS * PEER_NKEYS
PEER_DQ = 256
PEER_DHALF = PEER_DQ // 2
PEER_TOPK = 16
PEER_PAIRS = PEER_HEADS * PEER_TOPK
NORM_EPS = 1e-6
NEG_BIG = -1e30

LANES = 128
SUBLANES = 8
VMEM_LIMIT = 56 * 1024 * 1024


def _cparams(*sem):
    return pltpu.CompilerParams(dimension_semantics=sem, vmem_limit_bytes=VMEM_LIMIT)


def _rms(x, g):
    return x * lax.rsqrt(jnp.mean(x * x, axis=-1, keepdims=True) + NORM_EPS) * g


def _dot(a, b):
    return jnp.dot(a, b, preferred_element_type=F32)


def _dot_nt(a, b, precision=None):
    return lax.dot_general(a, b, (((1,), (1,)), ((), ())), precision=precision,
                           preferred_element_type=F32)


def _dot_hi(a, b):
    return jnp.dot(a, b, precision=HIGHEST, preferred_element_type=F32)


def _split(x):
    hi = x.astype(BF16)
    lo = (x - hi.astype(F32)).astype(BF16)
    return hi, lo


def _inproj_body(x_ref, g_ref, w_ref, wgh_ref, wgl_ref, qkv_ref, z_ref, sw_ref, gate_ref):
    y = _rms(x_ref[...], g_ref[...])
    yh, yl = _split(y)
    qkv_ref[...] = _dot(yh, w_ref[:, 0:CONV_DIM])
    z_ref[...] = _dot(yh, w_ref[:, CONV_DIM:CONV_DIM + DN_V])
    sw_ref[...] = _dot(yh, w_ref[:, CONV_DIM + DN_V:])
    gate_ref[...] = _dot(yh, wgh_ref[...]) + _dot(yl, wgh_ref[...]) + _dot(yh, wgl_ref[...])


def _in_proj(x, g, w, wgh, wgl, tm):
    n = x.shape[0]
    ncol = w.shape[1]
    row = lambda i: (i, 0)
    fix = lambda i: (0, 0)
    return pl.pallas_call(
        _inproj_body,
        grid=(n // tm,),
        in_specs=[pl.BlockSpec((tm, D_MODEL), row), pl.BlockSpec((1, D_MODEL), fix),
                  pl.BlockSpec((D_MODEL, ncol), fix), pl.BlockSpec((D_MODEL, LANES), fix),
                  pl.BlockSpec((D_MODEL, LANES), fix)],
        out_specs=[pl.BlockSpec((tm, CONV_DIM), row), pl.BlockSpec((tm, DN_V), row),
                   pl.BlockSpec((tm, SW_ALL), row), pl.BlockSpec((tm, LANES), row)],
        out_shape=[jax.ShapeDtypeStruct((n, CONV_DIM), F32), jax.ShapeDtypeStruct((n, DN_V), F32),
                   jax.ShapeDtypeStruct((n, SW_ALL), F32), jax.ShapeDtypeStruct((n, LANES), F32)],
        compiler_params=_cparams("arbitrary"),
        name="in_proj",
    )(x, g, w, wgh, wgl)


def _deltanet_body(qkv_ref, z_ref, gate_ref, cbuf_ref, s0_ref, cw_ref, alog_ref, dtb_ref, gon_ref,
                   o_ref, sfin_ref, state, tail, *, chunk, nchunk, t_valid):
    j = pl.program_id(1)
    tb = chunk * nchunk

    @pl.when(j == 0)
    def _():
        state[...] = s0_ref[0]
        tail[...] = cbuf_ref[0]

    x = qkv_ref[0]
    xp = jnp.concatenate([tail[...], x], axis=0)
    acc = x * cw_ref[CONV_W - 1:CONV_W, :]
    for jj in range(CONV_W - 1):
        sh = pltpu.roll(xp, CONV_W - 1 - jj, 0)[SUBLANES:, :]
        acc = acc + sh * cw_ref[jj:jj + 1, :]
    tail[...] = x[tb - SUBLANES:, :]
    act = acc * jax.nn.sigmoid(acc)

    gt = gate_ref[0]
    xg = gt + dtb_ref[...]
    softplus = jnp.maximum(xg, 0.0) + jnp.log1p(jnp.exp(-jnp.abs(xg)))
    gfull = -jnp.exp(alog_ref[...]) * softplus
    bfull = jax.nn.sigmoid(gt)
    if t_valid is not None:
        rowid = j * tb + lax.broadcasted_iota(jnp.int32, (tb, LANES), 0)
        live = rowid < t_valid
        gfull = jnp.where(live, gfull, 0.0)
        bfull = jnp.where(live, bfull, 0.0)
        act = jnp.where(live[:, 0:1], act, 0.0)

    ri = lax.broadcasted_iota(jnp.int32, (chunk, chunk), 0)
    ci = lax.broadcasted_iota(jnp.int32, (chunk, chunk), 1)
    causal = ri >= ci
    strict = ri > ci
    ltri = causal.astype(F32)
    eye = (ri == ci).astype(F32)
    lane = lax.broadcasted_iota(jnp.int32, (chunk, LANES), 1)
    ndbl = int(math.log2(chunk)) - 1

    prep = []
    for c in range(nchunk):
        rows = slice(c * chunk, (c + 1) * chunk)
        gc_all = _dot_hi(ltri, gfull[rows])
        per_head = []
        for h in range(DN_HEADS):
            q = act[rows, h * DN_DK:(h + 1) * DN_DK]
            k = act[rows, DN_QK + h * DN_DK:DN_QK + (h + 1) * DN_DK]
            v = act[rows, 2 * DN_QK + h * DN_DV:2 * DN_QK + (h + 1) * DN_DV]
            q = q * lax.rsqrt(jnp.sum(q * q, axis=-1, keepdims=True) + NORM_EPS) * (DN_DK ** -0.5)
            k = k * lax.rsqrt(jnp.sum(k * k, axis=-1, keepdims=True) + NORM_EPS)
            gcb = jnp.broadcast_to(gc_all[:, h:h + 1], (chunk, LANES))
            beta = jnp.broadcast_to(bfull[rows][:, DN_HEADS + h:DN_HEADS + h + 1], (chunk, LANES))
            da = jnp.where(lane == 0, gcb, jnp.where(lane == 1, 1.0, 0.0))
            db = jnp.where(lane == 0, 1.0, jnp.where(lane == 1, -gcb, 0.0))
            dmat = _dot_nt(da, db, HIGHEST)
            gam = jnp.where(causal, jnp.exp(jnp.where(causal, dmat, 0.0)), 0.0)
            eg = jnp.exp(gcb)
            glast = gcb[chunk - 1:chunk, :]
            kb = k * beta
            lmat = jnp.where(strict, _dot_nt(kb, k, HIGHEST) * gam, 0.0)
            inv = eye - lmat
            pw = _dot_hi(lmat, lmat)
            for it in range(ndbl):
                inv = inv + _dot_hi(inv, pw)
                if it + 1 < ndbl:
                    pw = _dot_hi(pw, pw)
            rhs = jnp.concatenate([v * beta, kb * eg], axis=-1)
            sol = _dot_hi(inv, rhs)
            aqk = jnp.where(causal, _dot_nt(q, k, HIGHEST) * gam, 0.0)
            per_head.append(dict(u0=sol[:, :DN_DV], wk=sol[:, DN_DV:], aqk=aqk, qg=q * eg,
                                 kd=k * jnp.exp(glast - gcb), dl=jnp.exp(glast)))
        prep.append(per_head)

    for c in range(nchunk):
        rows = slice(c * chunk, (c + 1) * chunk)
        for h in range(DN_HEADS):
            p = prep[c][h]
            s = state[h]
            u = p["u0"] - _dot_hi(p["wk"], s)
            o = _dot_hi(p["qg"], s) + _dot_hi(p["aqk"], u)
            state[h] = s * p["dl"] + lax.dot_general(
                p["kd"], u, (((0,), (0,)), ((), ())), precision=HIGHEST, preferred_element_type=F32)
            zz = z_ref[0, rows, h * DN_DV:(h + 1) * DN_DV]
            o_ref[0, rows, h * DN_DV:(h + 1) * DN_DV] = _rms(o, gon_ref[...]) * (zz * jax.nn.sigmoid(zz))

    @pl.when(j == pl.num_programs(1) - 1)
    def _():
        sfin_ref[0] = state[...]


def _deltanet(qkv, z, gates, cbuf8, s0, conv_w, alog_row, dtb_row, g_onorm, *, chunk, nchunk, t_valid):
    b, t, _ = qkv.shape
    tb = chunk * nchunk
    blk = lambda i, j: (i, j, 0)
    per_b3 = lambda i, j: (i, 0, 0)
    per_b4 = lambda i, j: (i, 0, 0, 0)
    fix = lambda i, j: (0, 0)
    return pl.pallas_call(
        functools.partial(_deltanet_body, chunk=chunk, nchunk=nchunk, t_valid=t_valid),
        grid=(b, t // tb),
        in_specs=[pl.BlockSpec((1, tb, CONV_DIM), blk), pl.BlockSpec((1, tb, DN_V), blk),
                  pl.BlockSpec((1, tb, LANES), blk), pl.BlockSpec((1, SUBLANES, CONV_DIM), per_b3),
                  pl.BlockSpec((1, DN_HEADS, DN_DK, DN_DV), per_b4),
                  pl.BlockSpec((CONV_W, CONV_DIM), fix), pl.BlockSpec((1, LANES), fix),
                  pl.BlockSpec((1, LANES), fix), pl.BlockSpec((1, DN_DV), fix)],
        out_specs=[pl.BlockSpec((1, tb, DN_V), blk), pl.BlockSpec((1, DN_HEADS, DN_DK, DN_DV), per_b4)],
        out_shape=[jax.ShapeDtypeStruct((b, t, DN_V), F32),
                   jax.ShapeDtypeStruct((b, DN_HEADS, DN_DK, DN_DV), F32)],
        scratch_shapes=[pltpu.VMEM((DN_HEADS, DN_DK, DN_DV), F32), pltpu.VMEM((SUBLANES, CONV_DIM), F32)],
        compiler_params=_cparams("arbitrary", "arbitrary"),
        name="deltanet",
    )(qkv, z, gates, cbuf8, s0, conv_w, alog_row, dtb_row, g_onorm)


def _softmax_heads(q, segs):
    outs, lses = [], []
    for h in range(SW_HEADS):
        cs = slice(h * SW_DH, (h + 1) * SW_DH)
        ss = [jnp.where(mk, _dot_nt(q[:, cs], k[:, cs]), NEG_BIG) for k, _, mk in segs]
        m = functools.reduce(jnp.maximum, [jnp.max(s, axis=-1, keepdims=True) for s in ss])
        es = [jnp.exp(s - m) for s in ss]
        l = functools.reduce(jnp.add, [jnp.sum(e, axis=-1, keepdims=True) for e in es])
        o = functools.reduce(jnp.add, [_dot((e / l).astype(BF16), v[:, cs]) for e, (_, v, _) in zip(es, segs)])
        outs.append(o)
        lses.append(jnp.broadcast_to(m + jnp.log(l), (q.shape[0], SW_DH)))
    return jnp.concatenate(outs, axis=-1), jnp.concatenate(lses, axis=-1)


def _dilp_body(q_ref, kc_ref, kp_ref, vc_ref, vp_ref, o_ref, l_ref):
    n = pl.program_id(2)
    q = (q_ref[0] * (SW_DH ** -0.5)).astype(BF16)
    qi = lax.broadcasted_iota(jnp.int32, (SW_BLOCK, SW_BLOCK), 0)
    ki = lax.broadcasted_iota(jnp.int32, (SW_BLOCK, SW_BLOCK), 1)
    mprev = (ki >= qi) & (n > 0)
    mcur = ki <= qi
    o, l = _softmax_heads(q, [(kp_ref[0].astype(BF16), vp_ref[0].astype(BF16), mprev),
                              (kc_ref[0].astype(BF16), vc_ref[0].astype(BF16), mcur)])
    o_ref[0] = o
    l_ref[0] = l


def _dilated_prompt(sw, gi, dil, batch, t):
    ln = t // dil
    nb = ln // SW_BLOCK
    npart = SW_ALL // SW_W
    swv = sw.reshape(batch, ln, dil * SW_ALL)
    cur = lambda part: (lambda b, r, n: (b, n, r * npart + 3 * gi + part))
    prev = lambda part: (lambda b, r, n: (b, jnp.maximum(n - 1, 0), r * npart + 3 * gi + part))
    blk = (1, SW_BLOCK, SW_W)
    outm = lambda b, r, n: (b, n, r)
    o, l = pl.pallas_call(
        _dilp_body,
        grid=(batch, dil, nb),
        in_specs=[pl.BlockSpec(blk, cur(0)), pl.BlockSpec(blk, cur(1)), pl.BlockSpec(blk, prev(1)),
                  pl.BlockSpec(blk, cur(2)), pl.BlockSpec(blk, prev(2))],
        out_specs=[pl.BlockSpec(blk, outm), pl.BlockSpec(blk, outm)],
        out_shape=[jax.ShapeDtypeStruct((batch, ln, dil * SW_W), F32)] * 2,
        compiler_params=_cparams("arbitrary", "arbitrary", "arbitrary"),
        name=f"dilated_prompt_{gi}",
    )(swv, swv, swv, swv, swv)
    return o.reshape(batch * t, SW_W), l.reshape(batch * t, SW_W)


def _dils_body(q_ref, kn_ref, vn_ref, buf_ref, o_ref, l_ref, *, win, dil, t_new, wb):
    tq = q_ref.shape[1]
    q = (q_ref[0] * (SW_DH ** -0.5)).astype(BF16)
    kb = buf_ref[0, :, 0:SW_W].astype(BF16)
    vb = buf_ref[0, :, SW_W:2 * SW_W].astype(BF16)

    def stride_mask(nkeys, first_pos):
        ti = lax.broadcasted_iota(jnp.int32, (tq, nkeys), 0)
        ji = first_pos + lax.broadcasted_iota(jnp.int32, (tq, nkeys), 1)
        d = wb + ti - ji
        return (d >= 0) & (d <= win) & ((d & (dil - 1)) == 0) & (ji < wb + t_new)

    o, l = _softmax_heads(q, [(kb, vb, stride_mask(wb, 0)),
                              (kn_ref[0].astype(BF16), vn_ref[0].astype(BF16), stride_mask(tq, wb))])
    o_ref[0] = o
    l_ref[0] = l


def _dilated_sample(q, kn, vn, buf, win, dil, t_new):
    b, tq, _ = q.shape
    wb = buf.shape[1]
    per = lambda i: (i, 0, 0)
    return pl.pallas_call(
        functools.partial(_dils_body, win=win, dil=dil, t_new=t_new, wb=wb),
        grid=(b,),
        in_specs=[pl.BlockSpec((1, tq, SW_W), per)] * 3 + [pl.BlockSpec((1, wb, 2 * SW_W), per)],
        out_specs=[pl.BlockSpec((1, tq, SW_W), per)] * 2,
        out_shape=[jax.ShapeDtypeStruct((b, tq, SW_W), F32)] * 2,
        compiler_params=_cparams("arbitrary"),
        name=f"dilated_sample_{win}",
    )(q, kn, vn, buf)


def _outproj_body(x_ref, odn_ref, o1, o2, o3, l1, l2, l3, wout_ref, g_ref, wmq_ref, h_ref, qm_ref):
    la, lb, lc = l1[...], l2[...], l3[...]
    m = jnp.maximum(jnp.maximum(la, lb), lc)
    ea, eb, ec = jnp.exp(la - m), jnp.exp(lb - m), jnp.exp(lc - m)
    osw = (ea * o1[...] + eb * o2[...] + ec * o3[...]) / (ea + eb + ec)
    mix = _dot(odn_ref[...].astype(BF16), wout_ref[0:DN_V, :]) + _dot(osw.astype(BF16), wout_ref[DN_V:, :])
    h = x_ref[...] + mix
    h_ref[...] = h
    c = _rms(h, g_ref[...]).astype(BF16)
    qm_ref[...] = (_dot(c, wmq_ref[...]) * (MEM_DH ** -0.5)).astype(BF16)


def _out_proj(x, odn, os_, ls_, wout, g_memq, wmq, tm):
    n = x.shape[0]
    row = lambda i: (i, 0)
    fix = lambda i: (0, 0)
    sws = pl.BlockSpec((tm, SW_W), row)
    return pl.pallas_call(
        _outproj_body,
        grid=(n // tm,),
        in_specs=[pl.BlockSpec((tm, D_MODEL), row), pl.BlockSpec((tm, DN_V), row)] + [sws] * 6 +
                 [pl.BlockSpec((DN_V + SW_W, D_MODEL), fix), pl.BlockSpec((1, D_MODEL), fix),
                  pl.BlockSpec((D_MODEL, D_MODEL), fix)],
        out_specs=[pl.BlockSpec((tm, D_MODEL), row), pl.BlockSpec((tm, D_MODEL), row)],
        out_shape=[jax.ShapeDtypeStruct((n, D_MODEL), F32), jax.ShapeDtypeStruct((n, D_MODEL), BF16)],
        compiler_params=_cparams("arbitrary"),
        name="out_proj",
    )(x, odn, *os_, *ls_, wout, g_memq, wmq)


def _memkv_body(m_ref, g_ref, w_ref, kv_ref):
    kv_ref[...] = _dot(_rms(m_ref[...], g_ref[...]).astype(BF16), w_ref[...])


def _memory_kv(mem, g, w, tm=256, tn=1024):
    n = mem.shape[0]
    ncol = w.shape[1]
    return pl.pallas_call(
        _memkv_body,
        grid=(n // tm, ncol // tn),
        in_specs=[pl.BlockSpec((tm, D_MODEL), lambda i, j: (i, 0)), pl.BlockSpec((1, D_MODEL), lambda i, j: (0, 0)),
                  pl.BlockSpec((D_MODEL, tn), lambda i, j: (0, j))],
        out_specs=pl.BlockSpec((tm, tn), lambda i, j: (i, j)),
        out_shape=jax.ShapeDtypeStruct((n, ncol), F32),
        compiler_params=_cparams("arbitrary", "arbitrary"),
        name="memory_kv",
    )(mem, g, w)


def _memattn_body(h_ref, q_ref, k_ref, v_ref, wmo_ref, o_ref):
    q = q_ref[0]
    outs = []
    for hd in range(MEM_HEADS):
        cs = slice(hd * MEM_DH, (hd + 1) * MEM_DH)
        s = _dot_nt(q[:, cs], k_ref[0, :, cs])
        m = jnp.max(s, axis=-1, keepdims=True)
        e = jnp.exp(s - m)
        p = e / jnp.sum(e, axis=-1, keepdims=True)
        outs.append(_dot(p.astype(BF16), v_ref[0, :, cs]))
    o = jnp.concatenate(outs, axis=-1).astype(BF16)
    o_ref[0] = h_ref[0] + _dot(o, wmo_ref[...])


def _mem_attend(h, qm, kvb, wmo, tm):
    b, t, _ = h.shape
    blk = lambda i, j: (i, j, 0)
    return pl.pallas_call(
        _memattn_body,
        grid=(b, t // tm),
        in_specs=[pl.BlockSpec((1, tm, D_MODEL), blk), pl.BlockSpec((1, tm, D_MODEL), blk),
                  pl.BlockSpec((1, MEM_LEN, D_MODEL), lambda i, j: (i, 0, 0)),
                  pl.BlockSpec((1, MEM_LEN, D_MODEL), lambda i, j: (i, 0, 1)),
                  pl.BlockSpec((D_MODEL, D_MODEL), lambda i, j: (0, 0))],
        out_specs=pl.BlockSpec((1, tm, D_MODEL), blk),
        out_shape=jax.ShapeDtypeStruct((b, t, D_MODEL), F32),
        compiler_params=_cparams("arbitrary", "arbitrary"),
        name="mem_attend",
    )(h, qm, kvb, kvb, wmo)


def _topk_rows(s, k, payload=None):
    r = s.shape[0]
    rid = lax.broadcasted_iota(jnp.int32, s.shape, 0).astype(F32)
    vals, picks = [], []
    for _ in range(k):
        m = jnp.max(s, axis=0, keepdims=True)
        idx = jnp.min(jnp.where(s == m, rid, float(r)), axis=0, keepdims=True)
        sel = rid == idx
        vals.append(m)
        if payload is None:
            picks.append(idx)
        else:
            picks.append(jnp.sum(jnp.where(sel, payload, 0.0), axis=0, keepdims=True))
        s = jnp.where(sel, -jnp.inf, s)
    return vals, picks


def _route_body(h_ref, g_ref, wh_ref, wl_ref, kh_ref, kl_ref, f_ref, eid_ref, gate_ref):
    f = _rms(h_ref[...], g_ref[...])
    f_ref[...] = f
    fh, fl = _split(f)
    qt = _dot_nt(wh_ref[...], fh) + _dot_nt(wh_ref[...], fl) + _dot_nt(wl_ref[...], fh)
    eids, gates = [], []
    for hd in range(PEER_HEADS):
        sv, si = [], []
        for p in range(2):
            hp = 2 * hd + p
            qh, ql = _split(qt[hp * PEER_DHALF:(hp + 1) * PEER_DHALF, :])
            s = _dot(kh_ref[hp], qh) + _dot(kh_ref[hp], ql) + _dot(kl_ref[hp], qh)
            v, i = _topk_rows(s, PEER_TOPK)
            sv.append(v)
            si.append(i)
        sv2 = jnp.concatenate(sv[1], axis=0)
        si2 = jnp.concatenate(si[1], axis=0)
        cand = jnp.concatenate([sv[0][a] + sv2 for a in range(PEER_TOPK)], axis=0)
        cid = jnp.concatenate([si[0][a] * PEER_NKEYS + si2 for a in range(PEER_TOPK)], axis=0)
        cv, ce = _topk_rows(cand, PEER_TOPK, payload=cid)
        cv = jnp.concatenate(cv, axis=0)
        ex = jnp.exp(cv - cv[0:1, :])
        gates.append(ex / jnp.sum(ex, axis=0, keepdims=True))
        eids.append(jnp.concatenate(ce, axis=0))
    eid_ref[...] = jnp.concatenate(eids, axis=0).astype(jnp.int32)
    gate_ref[...] = jnp.concatenate(gates, axis=0)


def _peer_route(h, g, wpq_h, wpq_l, keys_h, keys_l, tm):
    n = h.shape[0]
    row = lambda i: (i, 0)
    col = lambda i: (0, i)
    fix2 = lambda i: (0, 0)
    fix3 = lambda i: (0, 0, 0)
    nq = PEER_HEADS * PEER_DQ
    return pl.pallas_call(
        _route_body,
        grid=(n // tm,),
        in_specs=[pl.BlockSpec((tm, D_MODEL), row), pl.BlockSpec((1, D_MODEL), fix2),
                  pl.BlockSpec((nq, D_MODEL), fix2), pl.BlockSpec((nq, D_MODEL), fix2),
                  pl.BlockSpec((2 * PEER_HEADS, PEER_NKEYS, PEER_DHALF), fix3),
                  pl.BlockSpec((2 * PEER_HEADS, PEER_NKEYS, PEER_DHALF), fix3)],
        out_specs=[pl.BlockSpec((tm, D_MODEL), row), pl.BlockSpec((PEER_PAIRS, tm), col),
                   pl.BlockSpec((PEER_PAIRS, tm), col)],
        out_shape=[jax.ShapeDtypeStruct((n, D_MODEL), F32), jax.ShapeDtypeStruct((PEER_PAIRS, n), jnp.int32),
                   jax.ShapeDtypeStruct((PEER_PAIRS, n), F32)],
        compiler_params=_cparams("arbitrary"),
        name="peer_route",
    )(h, g, wpq_h, wpq_l, keys_h, keys_l)


def _load_table(tbl_hbm, tbl, sem):
    @pl.when(pl.program_id(0) == 0)
    def _():
        cp = pltpu.make_async_copy(tbl_hbm, tbl, sem)
        cp.start()
        cp.wait()


def _unpack(w):
    return (lax.bitcast_convert_type(w << 16, F32),
            lax.bitcast_convert_type(w & jnp.int32(-65536), F32))


def _peer_hid_body(eid_ref, f_ref, gate_ref, tbl_hbm, w_ref, tbl, sem, *, te):
    _load_table(tbl_hbm, tbl, sem)
    half = SUBLANES // 2
    rid = lax.broadcasted_iota(jnp.int32, (PEER_PAIRS * half, PEER_PAIRS), 0) // half
    lid = lax.broadcasted_iota(jnp.int32, (PEER_PAIRS * half, PEER_PAIRS), 1)
    msk = rid == lid

    def tok(t, carry):
        x = f_ref[t]
        xlo, xhi = x[0:half], x[half:SUBLANES]
        parts = []
        for k in range(PEER_PAIRS):
            lo, hi = _unpack(tbl[eid_ref[t, k]])
            parts.append(lo * xlo + hi * xhi)
        r = jnp.sum(jnp.concatenate(parts, axis=0), axis=-1, keepdims=True)
        w_ref[pl.ds(t, 1), :] = jnp.sum(jnp.where(msk, r, 0.0), axis=0, keepdims=True)
        return carry

    lax.fori_loop(0, te, tok, 0)
    hid = w_ref[...]
    c = math.sqrt(2.0 / math.pi)
    gelu = 0.5 * hid * (1.0 + jnp.tanh(c * (hid + 0.044715 * (hid * hid * hid))))
    w_ref[...] = gate_ref[...] * gelu


def _peer_hid(eid, f3, gate, tbl, te):
    n = eid.shape[0]
    row = lambda i: (i, 0)
    return pl.pallas_call(
        functools.partial(_peer_hid_body, te=te),
        grid=(n // te,),
        in_specs=[pl.BlockSpec((te, PEER_PAIRS), row, memory_space=pltpu.SMEM),
                  pl.BlockSpec((te, SUBLANES, LANES), lambda i: (i, 0, 0)),
                  pl.BlockSpec((te, PEER_PAIRS), row),
                  pl.BlockSpec(memory_space=pl.ANY)],
        out_specs=pl.BlockSpec((te, PEER_PAIRS), row),
        out_shape=jax.ShapeDtypeStruct((n, PEER_PAIRS), F32),
        scratch_shapes=[pltpu.VMEM((PEER_N, SUBLANES // 2, LANES), jnp.int32), pltpu.SemaphoreType.DMA],
        compiler_params=_cparams("arbitrary"),
        name="peer_hid",
    )(eid, f3, gate, tbl)


def _peer_out_body(eid_ref, w_ref, h_ref, tbl_hbm, o_ref, tbl, sem, *, te):
    _load_table(tbl_hbm, tbl, sem)
    nacc = 4

    def tok(t, carry):
        lo_acc = [None] * nacc
        hi_acc = [None] * nacc
        for k in range(PEER_PAIRS):
            lo, hi = _unpack(tbl[eid_ref[t, k]])
            wk = w_ref[t, k]
            a = k % nacc
            lo_acc[a] = wk * lo if lo_acc[a] is None else lo_acc[a] + wk * lo
            hi_acc[a] = wk * hi if hi_acc[a] is None else hi_acc[a] + wk * hi
        lo = (lo_acc[0] + lo_acc[1]) + (lo_acc[2] + lo_acc[3])
        hi = (hi_acc[0] + hi_acc[1]) + (hi_acc[2] + hi_acc[3])
        o_ref[t] = h_ref[t] + jnp.concatenate([lo, hi], axis=0)
        return carry

    lax.fori_loop(0, te, tok, 0)


def _peer_out(eid, w, h3, tbl, te):
    n = eid.shape[0]
    row = lambda i: (i, 0)
    row3 = lambda i: (i, 0, 0)
    return pl.pallas_call(
        functools.partial(_peer_out_body, te=te),
        grid=(n // te,),
        in_specs=[pl.BlockSpec((te, PEER_PAIRS), row, memory_space=pltpu.SMEM),
                  pl.BlockSpec((te, PEER_PAIRS), row, memory_space=pltpu.SMEM),
                  pl.BlockSpec((te, SUBLANES, LANES), row3),
                  pl.BlockSpec(memory_space=pl.ANY)],
        out_specs=pl.BlockSpec((te, SUBLANES, LANES), row3),
        out_shape=jax.ShapeDtypeStruct((n, SUBLANES, LANES), F32),
        scratch_shapes=[pltpu.VMEM((PEER_N, SUBLANES // 2, LANES), jnp.int32), pltpu.SemaphoreType.DMA],
        compiler_params=_cparams("arbitrary"),
        name="peer_out",
    )(eid, w, h3, tbl)


def _final_body(h_ref, g_ref, y_ref):
    y_ref[...] = _rms(h_ref[...], g_ref[...])


def _final_norm(h, g, tm):
    n = h.shape[0]
    return pl.pallas_call(
        _final_body,
        grid=(n // tm,),
        in_specs=[pl.BlockSpec((tm, D_MODEL), lambda i: (i, 0)), pl.BlockSpec((1, D_MODEL), lambda i: (0, 0))],
        out_specs=pl.BlockSpec((tm, D_MODEL), lambda i: (i, 0)),
        out_shape=jax.ShapeDtypeStruct((n, D_MODEL), F32),
        compiler_params=_cparams("arbitrary"),
        name="final_norm",
    )(h, g)


def _pack_table(t):
    bits = lax.bitcast_convert_type(t.astype(BF16), jnp.uint16).astype(jnp.uint32)
    half = D_MODEL // 2
    words = bits[:, :half] | (bits[:, half:] << 16)
    return lax.bitcast_convert_type(words, jnp.int32).reshape(t.shape[0], SUBLANES // 2, LANES)


def _prep_weights(g_mix, w_in, conv_w, a_log, dt_bias, g_onorm, w_out, g_memq, g_memkv, w_mq, w_mkv, w_mo,
                  g_ffn, w_pq, sub_keys, expert_u, expert_v, g_final):
    c0 = 2 * DN_QK + DN_V
    gate_cols = w_in[:, c0:c0 + 2 * DN_HEADS]
    z_cols = w_in[:, c0 + 2 * DN_HEADS:c0 + 2 * DN_HEADS + DN_V]
    sw_cols = w_in[:, c0 + 2 * DN_HEADS + DN_V:]
    w_main = jnp.concatenate([w_in[:, :c0], z_cols, sw_cols], axis=1).astype(BF16)
    wg = jnp.pad(gate_cols, ((0, 0), (0, LANES - 2 * DN_HEADS)))
    wgh, wgl = _split(wg)
    wpq_t = w_pq.T
    wpq_h, wpq_l = _split(wpq_t)
    keys = sub_keys.reshape(2 * PEER_HEADS, PEER_NKEYS, PEER_DHALF)
    keys_h, keys_l = _split(keys)
    pad_row = lambda v: jnp.pad(v.astype(F32), (0, LANES - v.shape[0])).reshape(1, LANES)
    return dict(
        g_mix=g_mix.reshape(1, D_MODEL), w_main=w_main, wgh=wgh, wgl=wgl, conv_w=conv_w,
        alog_row=pad_row(a_log), dtb_row=pad_row(dt_bias), g_onorm=g_onorm.reshape(1, DN_DV),
        w_out=w_out.astype(BF16), g_memq=g_memq.reshape(1, D_MODEL), g_memkv=g_memkv.reshape(1, D_MODEL),
        w_mq=w_mq.astype(BF16), w_mkv=w_mkv.astype(BF16), w_mo=w_mo.astype(BF16),
        g_ffn=g_ffn.reshape(1, D_MODEL), wpq_h=wpq_h, wpq_l=wpq_l, keys_h=keys_h, keys_l=keys_l,
        tbl_u=_pack_table(expert_u), tbl_v=_pack_table(expert_v), g_final=g_final.reshape(1, D_MODEL))


def _layer(x, cbuf8, s0, wins, kvb, w, *, chunk, nchunk, t_valid, t_new):
    b, t, _ = x.shape
    n = b * t
    tm = min(256, n)
    x2 = x.reshape(n, D_MODEL)
    qkv, z, sw, gates = _in_proj(x2, w["g_mix"], w["w_main"], w["wgh"], w["wgl"], tm)
    odn, s_new = _deltanet(qkv.reshape(b, t, CONV_DIM), z.reshape(b, t, DN_V), gates.reshape(b, t, LANES),
                           cbuf8, s0, w["conv_w"], w["alog_row"], w["dtb_row"], w["g_onorm"],
                           chunk=chunk, nchunk=nchunk, t_valid=t_valid)
    outs, lses = [], []
    for gi, (win, dil) in enumerate(SW_GROUPS):
        if wins is None:
            o, l = _dilated_prompt(sw, gi, dil, b, t)
        else:
            part = lambda p: sw[:, (3 * gi + p) * SW_W:(3 * gi + p + 1) * SW_W].reshape(b, t, SW_W)
            o, l = _dilated_sample(part(0), part(1), part(2), wins[gi], win, dil, t_new)
            o, l = o.reshape(n, SW_W), l.reshape(n, SW_W)
        outs.append(o)
        lses.append(l)
    h1, qm = _out_proj(x2, odn.reshape(n, DN_V), outs, lses, w["w_out"], w["g_memq"], w["w_mq"], tm)
    tma = min(512, t)
    h2 = _mem_attend(h1.reshape(b, t, D_MODEL), qm.reshape(b, t, D_MODEL), kvb, w["w_mo"], tma)
    h2 = h2.reshape(n, D_MODEL)
    f, eid_t, gate_t = _peer_route(h2, w["g_ffn"], w["wpq_h"], w["wpq_l"], w["keys_h"], w["keys_l"], tm)
    eid, gate = eid_t.T, gate_t.T
    te = min(128, n)
    wgt = _peer_hid(eid, f.reshape(n, SUBLANES, LANES), gate, w["tbl_u"], te)
    h3 = _peer_out(eid, wgt, h2.reshape(n, SUBLANES, LANES), w["tbl_v"], te)
    y = _final_norm(h3.reshape(n, D_MODEL), w["g_final"], tm)
    return y.reshape(b, t, D_MODEL), qkv.reshape(b, t, CONV_DIM), s_new, sw.reshape(b, t, SW_ALL)


def kernel(x_prompt, x_sample, state_delta, state_conv, cache_win1, cache_win2, cache_win3, cache_mem_kv,
           mem_prompt, g_mix, w_in, conv_w, a_log, dt_bias, g_onorm, w_out, g_memq, g_memkv, w_mq, w_mkv,
           w_mo, g_ffn, w_pq, sub_keys, expert_u, expert_v, g_final):
    w = _prep_weights(g_mix[0], w_in[0], conv_w[0], a_log[0], dt_bias[0], g_onorm[0], w_out[0], g_memq[0],
                      g_memkv[0], w_mq[0], w_mkv[0], w_mo[0], g_ffn[0], w_pq[0], sub_keys[0], expert_u[0],
                      expert_v[0], g_final)
    bp, tp, _ = x_prompt.shape
    bs, ts, _ = x_sample.shape

    kv_p = _memory_kv(mem_prompt.reshape(bp * MEM_LEN, D_MODEL), w["g_memkv"], w["w_mkv"])
    kv_p3 = kv_p.reshape(bp, MEM_LEN, 2 * D_MODEL)
    y_p, qkv_p, sdelta_p, sw_p = _layer(
        x_prompt, jnp.zeros((bp, SUBLANES, CONV_DIM), F32), jnp.zeros((bp, DN_HEADS, DN_DK, DN_DV), F32),
        None, kv_p3.astype(BF16), w, chunk=64, nchunk=2, t_valid=None, t_new=None)
    p_conv = qkv_p[:, tp - (CONV_W - 1):]
    p_wins = []
    for gi, (win, _) in enumerate(SW_GROUPS):
        keep = min(win, tp)
        kvrows = sw_p[:, tp - keep:].reshape(bp, keep, len(SW_GROUPS), 3, SW_HEADS, SW_DH)
        p_wins.append(kvrows[:, :, gi, 1:3])
    p_mem = kv_p.reshape(bp, MEM_LEN, 2, MEM_HEADS, MEM_DH)

    tpad = SUBLANES
    xs = jnp.pad(x_sample, ((0, 0), (0, tpad - ts), (0, 0)))
    cbuf8 = jnp.pad(state_conv[0], ((0, 0), (SUBLANES - (CONV_W - 1), 0), (0, 0)))
    wins = [c[0].reshape(bs, c.shape[2], 2 * SW_W) for c in (cache_win1, cache_win2, cache_win3)]
    kv_s = cache_mem_kv[0].reshape(bs, MEM_LEN, 2 * D_MODEL).astype(BF16)
    y_s, qkv_s, sdelta_s, sw_s = _layer(xs, cbuf8, state_delta[0], wins, kv_s, w,
                                        chunk=SUBLANES, nchunk=1, t_valid=ts, t_new=ts)
    y_s = y_s[:, :ts]
    s_conv = jnp.concatenate([state_conv[0], qkv_s[:, :ts]], axis=1)[:, -(CONV_W - 1):]
    s_wins = []
    for gi, (win, _) in enumerate(SW_GROUPS):
        new = sw_s[:, :ts].reshape(bs, ts, len(SW_GROUPS), 3, SW_HEADS, SW_DH)[:, :, gi, 1:3]
        cache = (cache_win1, cache_win2, cache_win3)[gi][0]
        full = jnp.concatenate([cache, new], axis=1)
        s_wins.append(full[:, -min(win, full.shape[1]):])

    st = lambda a: a[None]
    return (y_p, y_s, st(sdelta_p), st(p_conv), st(p_wins[0]), st(p_wins[1]), st(p_wins[2]), st(p_mem),
            st(sdelta_s), st(s_conv), st(s_wins[0]), st(s_wins[1]), st(s_wins[2]))
```

```python
import functools
import math

import jax
import jax.numpy as jnp
from jax import lax
from jax.experimental import pallas as pl
from jax.experimental.pallas import tpu as pltpu

F32 = jnp.float32
BF16 = jnp.bfloat16

D_MODEL = 1024
DN_HEADS = 4
DN_DK = 128
DN_DV = 128
CONV_W = 4
DN_QK = DN_HEADS * DN_DK
DN_V = DN_HEADS * DN_DV
CONV_DIM = 2 * DN_QK + DN_V
SW_GROUPS = ((128, 1), (512, 4), (2048, 16))
SW_HEADS = 4
SW_DH = 64
SW_W = SW_HEADS * SW_DH
SW_BLOCK = 128
SW_ALL = 3 * len(SW_GROUPS) * SW_W
MEM_LEN = 256
MEM_HEADS = 4
MEM_DH = D_MODEL // MEM_HEADS
PEER_HEADS = 8
PEER_NKEYS = 128
PEER_N = PEER_NKEYS * PEER_NKEYS
PEER_DQ = 256
PEER_DHALF = PEER_DQ // 2
PEER_TOPK = 16
PEER_PAIRS = PEER_HEADS * PEER_TOPK
NORM_EPS = 1e-6
NEG_BIG = -1e30

LANES = 128
SUBLANES = 8
VMEM_LIMIT = 56 * 1024 * 1024


def _cparams(*sem):
    return pltpu.CompilerParams(dimension_semantics=sem, vmem_limit_bytes=VMEM_LIMIT)


def _rms(x, g):
    return x * lax.rsqrt(jnp.mean(x * x, axis=-1, keepdims=True) + NORM_EPS) * g


def _dot(a, b):
    return jnp.dot(a, b, preferred_element_type=F32)


def _dot_nt(a, b):
    return lax.dot_general(a, b, (((1,), (1,)), ((), ())), preferred_element_type=F32)


def _dot_tn(a, b):
    return lax.dot_general(a, b, (((0,), (0,)), ((), ())), preferred_element_type=F32)


def _dot3(a, b, dot=_dot):
    ah, al = _split(a)
    bh, bl = _split(b)
    return dot(ah, bh) + dot(al, bh) + dot(ah, bl)


def _split(x):
    hi = x.astype(BF16)
    lo = (x - hi.astype(F32)).astype(BF16)
    return hi, lo


def _inproj_body(x_ref, g_ref, w_ref, wgh_ref, wgl_ref, qkv_ref, z_ref, sw_ref, gate_ref):
    y = _rms(x_ref[...], g_ref[...])
    yh, yl = _split(y)
    qkv_ref[...] = _dot(yh, w_ref[:, 0:CONV_DIM])
    z_ref[...] = _dot(yh, w_ref[:, CONV_DIM:CONV_DIM + DN_V])
    sw_ref[...] = _dot(yh, w_ref[:, CONV_DIM + DN_V:])
    gate_ref[...] = _dot(yh, wgh_ref[...]) + _dot(yl, wgh_ref[...]) + _dot(yh, wgl_ref[...])


def _in_proj(x, g, w, wgh, wgl, tm):
    n = x.shape[0]
    ncol = w.shape[1]
    row = lambda i: (i, 0)
    fix = lambda i: (0, 0)
    return pl.pallas_call(
        _inproj_body,
        grid=(n // tm,),
        in_specs=[pl.BlockSpec((tm, D_MODEL), row), pl.BlockSpec((1, D_MODEL), fix),
                  pl.BlockSpec((D_MODEL, ncol), fix), pl.BlockSpec((D_MODEL, LANES), fix),
                  pl.BlockSpec((D_MODEL, LANES), fix)],
        out_specs=[pl.BlockSpec((tm, CONV_DIM), row), pl.BlockSpec((tm, DN_V), row),
                   pl.BlockSpec((tm, SW_ALL), row), pl.BlockSpec((tm, LANES), row)],
        out_shape=[jax.ShapeDtypeStruct((n, CONV_DIM), F32), jax.ShapeDtypeStruct((n, DN_V), F32),
                   jax.ShapeDtypeStruct((n, SW_ALL), F32), jax.ShapeDtypeStruct((n, LANES), F32)],
        compiler_params=_cparams("arbitrary"),
        name="in_proj",
    )(x, g, w, wgh, wgl)


def _deltanet_body(qkv_ref, z_ref, gate_ref, cbuf_ref, s0_ref, cw_ref, alog_ref, dtb_ref, gon_ref,
                   o_ref, sfin_ref, state, tail, *, chunk, nchunk, t_valid):
    j = pl.program_id(1)
    tb = chunk * nchunk

    @pl.when(j == 0)
    def _():
        state[...] = s0_ref[0]
        tail[...] = cbuf_ref[0]

    x = qkv_ref[0]
    xp = jnp.concatenate([tail[...], x], axis=0)
    acc = x * cw_ref[CONV_W - 1:CONV_W, :]
    for jj in range(CONV_W - 1):
        sh = pltpu.roll(xp, CONV_W - 1 - jj, 0)[SUBLANES:, :]
        acc = acc + sh * cw_ref[jj:jj + 1, :]
    tail[...] = x[tb - SUBLANES:, :]
    act = acc * jax.nn.sigmoid(acc)

    gt = gate_ref[0]
    xg = gt + dtb_ref[...]
    softplus = jnp.maximum(xg, 0.0) + jnp.log1p(jnp.exp(-jnp.abs(xg)))
    gfull = -jnp.exp(alog_ref[...]) * softplus
    bfull = jax.nn.sigmoid(gt)
    if t_valid is not None:
        rowid = j * tb + lax.broadcasted_iota(jnp.int32, (tb, LANES), 0)
        live = rowid < t_valid
        gfull = jnp.where(live, gfull, 0.0)
        bfull = jnp.where(live, bfull, 0.0)
        act = jnp.where(live[:, 0:1], act, 0.0)

    nh = DN_HEADS
    rows_all = nh * chunk
    ri = lax.broadcasted_iota(jnp.int32, (rows_all, rows_all), 0)
    ci = lax.broadcasted_iota(jnp.int32, (rows_all, rows_all), 1)
    same = (ri // chunk) == (ci // chunk)
    causal = same & (ri >= ci)
    strict = same & (ri > ci)
    eye = (ri == ci).astype(F32)
    ti = lax.broadcasted_iota(jnp.int32, (chunk, chunk), 0)
    tj = lax.broadcasted_iota(jnp.int32, (chunk, chunk), 1)
    ltri = (ti >= tj).astype(F32)
    lane = lax.broadcasted_iota(jnp.int32, (rows_all, LANES), 1)
    hrow = lax.broadcasted_iota(jnp.int32, (rows_all, nh * DN_DK), 0) // chunk
    hcol = lax.broadcasted_iota(jnp.int32, (rows_all, nh * DN_DK), 1) // DN_DK
    own = hrow == hcol
    ndbl = int(math.log2(chunk)) - 1

    def stack(f):
        return jnp.concatenate([f(h) for h in range(nh)], axis=0)

    def widen(a):
        return jnp.where(own, jnp.concatenate([a] * nh, axis=1), 0.0)

    prep = []
    for c in range(nchunk):
        rows = slice(c * chunk, (c + 1) * chunk)
        gc_all = _dot3(ltri, gfull[rows])
        bsl = bfull[rows]
        a_c = act[rows]

        def unit(x):
            return x * lax.rsqrt(jnp.sum(x * x, axis=-1, keepdims=True) + NORM_EPS)

        q = stack(lambda h: unit(a_c[:, h * DN_DK:(h + 1) * DN_DK])) * (DN_DK ** -0.5)
        k = stack(lambda h: unit(a_c[:, DN_QK + h * DN_DK:DN_QK + (h + 1) * DN_DK]))
        v = stack(lambda h: a_c[:, 2 * DN_QK + h * DN_DV:2 * DN_QK + (h + 1) * DN_DV])
        gcb = stack(lambda h: jnp.broadcast_to(gc_all[:, h:h + 1], (chunk, LANES)))
        beta = stack(lambda h: jnp.broadcast_to(bsl[:, nh + h:nh + h + 1], (chunk, LANES)))
        glast = stack(lambda h: jnp.broadcast_to(gc_all[chunk - 1:chunk, h:h + 1], (chunk, LANES)))
        da = jnp.where(lane == 0, gcb, jnp.where(lane == 1, 1.0, 0.0))
        db = jnp.where(lane == 0, 1.0, jnp.where(lane == 1, -gcb, 0.0))
        dmat = _dot3(da, db, _dot_nt)
        gam = jnp.where(causal, jnp.exp(jnp.where(causal, dmat, 0.0)), 0.0)
        eg = jnp.exp(gcb)
        kb = k * beta
        lmat = jnp.where(strict, _dot3(kb, k, _dot_nt) * gam, 0.0)
        inv = eye - lmat
        pw = _dot3(lmat, lmat)
        for it in range(ndbl):
            inv = inv + _dot3(inv, pw)
            if it + 1 < ndbl:
                pw = _dot3(pw, pw)
        sol = _dot3(inv, jnp.concatenate([v * beta, kb * eg], axis=-1))
        aqk = jnp.where(causal, _dot3(q, k, _dot_nt) * gam, 0.0)
        dl = stack(lambda h: jnp.broadcast_to(jnp.exp(gc_all[chunk - 1:chunk, h:h + 1]), (DN_DK, DN_DV)))
        prep.append(dict(u0=sol[:, :DN_DV], wk=widen(sol[:, DN_DV:]), aqk=aqk, qg=widen(q * eg),
                         kd=widen(k * jnp.exp(glast - gcb)), dl=dl))

    for c in range(nchunk):
        rows = slice(c * chunk, (c + 1) * chunk)
        p = prep[c]
        s = state[...]
        u = p["u0"] - _dot3(p["wk"], s)
        o = _dot3(p["qg"], s) + _dot3(p["aqk"], u)
        state[...] = s * p["dl"] + _dot3(p["kd"], u, _dot_tn)
        for h in range(nh):
            zz = z_ref[0, rows, h * DN_DV:(h + 1) * DN_DV]
            o_ref[0, rows, h * DN_DV:(h + 1) * DN_DV] = (
                _rms(o[h * chunk:(h + 1) * chunk], gon_ref[...]) * (zz * jax.nn.sigmoid(zz)))

    @pl.when(j == pl.num_programs(1) - 1)
    def _():
        sfin_ref[0] = state[...]


def _deltanet(qkv, z, gates, cbuf8, s0, conv_w, alog_row, dtb_row, g_onorm, *, chunk, nchunk, t_valid):
    b, t, _ = qkv.shape
    tb = chunk * nchunk
    blk = lambda i, j: (i, j, 0)
    per_b3 = lambda i, j: (i, 0, 0)
    fix = lambda i, j: (0, 0)
    odn, s_new = pl.pallas_call(
        functools.partial(_deltanet_body, chunk=chunk, nchunk=nchunk, t_valid=t_valid),
        grid=(b, t // tb),
        in_specs=[pl.BlockSpec((1, tb, CONV_DIM), blk), pl.BlockSpec((1, tb, DN_V), blk),
                  pl.BlockSpec((1, tb, LANES), blk), pl.BlockSpec((1, SUBLANES, CONV_DIM), per_b3),
                  pl.BlockSpec((1, DN_HEADS * DN_DK, DN_DV), per_b3),
                  pl.BlockSpec((CONV_W, CONV_DIM), fix), pl.BlockSpec((1, LANES), fix),
                  pl.BlockSpec((1, LANES), fix), pl.BlockSpec((1, DN_DV), fix)],
        out_specs=[pl.BlockSpec((1, tb, DN_V), blk), pl.BlockSpec((1, DN_HEADS * DN_DK, DN_DV), per_b3)],
        out_shape=[jax.ShapeDtypeStruct((b, t, DN_V), F32),
                   jax.ShapeDtypeStruct((b, DN_HEADS * DN_DK, DN_DV), F32)],
        scratch_shapes=[pltpu.VMEM((DN_HEADS * DN_DK, DN_DV), F32), pltpu.VMEM((SUBLANES, CONV_DIM), F32)],
        compiler_params=_cparams("arbitrary", "arbitrary"),
        name="deltanet",
    )(qkv, z, gates, cbuf8, s0.reshape(b, DN_HEADS * DN_DK, DN_DV), conv_w, alog_row, dtb_row, g_onorm)
    return odn, s_new.reshape(b, DN_HEADS, DN_DK, DN_DV)


def _softmax_heads(q, segs):
    outs, lses = [], []
    for h in range(q.shape[1] // SW_DH):
        cs = slice(h * SW_DH, (h + 1) * SW_DH)
        ss = [jnp.where(mk, _dot_nt(q[:, cs], k[:, cs]), NEG_BIG) for k, _, mk in segs]
        m = functools.reduce(jnp.maximum, [jnp.max(s, axis=-1, keepdims=True) for s in ss])
        es = [jnp.exp(s - m) for s in ss]
        l = functools.reduce(jnp.add, [jnp.sum(e, axis=-1, keepdims=True) for e in es])
        o = functools.reduce(jnp.add, [_dot((e / l).astype(BF16), v[:, cs]) for e, (_, v, _) in zip(es, segs)])
        outs.append(o)
        lses.append(jnp.broadcast_to(m + jnp.log(l), (q.shape[0], SW_DH)))
    return jnp.concatenate(outs, axis=-1), jnp.concatenate(lses, axis=-1)


def _dilp_body(q_ref, kc_ref, kp_ref, vc_ref, vp_ref, o_ref, l_ref, *, dil):
    n = pl.program_id(1)
    qi = lax.broadcasted_iota(jnp.int32, (SW_BLOCK, SW_BLOCK), 0)
    ki = lax.broadcasted_iota(jnp.int32, (SW_BLOCK, SW_BLOCK), 1)
    mprev = (ki >= qi) & (n > 0)
    mcur = ki <= qi

    def residue(r, carry):
        rows = pl.ds(r, SW_BLOCK, stride=dil) if dil > 1 else slice(None)
        q = (q_ref[0, rows, :] * (SW_DH ** -0.5)).astype(BF16)
        o, l = _softmax_heads(q, [(kp_ref[0, rows, :].astype(BF16), vp_ref[0, rows, :].astype(BF16), mprev),
                                  (kc_ref[0, rows, :].astype(BF16), vc_ref[0, rows, :].astype(BF16), mcur)])
        o_ref[0, rows, :] = o
        l_ref[0, rows, :] = l
        return carry

    if dil > 1:
        lax.fori_loop(0, dil, residue, 0)
    else:
        residue(0, 0)


def _dilated_prompt(sw, gi, dil, batch, t):
    sb = SW_BLOCK * dil
    sw3 = sw.reshape(batch, t, SW_ALL)
    nhalf = SW_W // LANES
    cur = lambda part: (lambda b, n, hf: (b, n, nhalf * (3 * gi + part) + hf))
    prev = lambda part: (lambda b, n, hf: (b, jnp.maximum(n - 1, 0), nhalf * (3 * gi + part) + hf))
    blk = (1, sb, LANES)
    outm = lambda b, n, hf: (b, n, hf)
    o, l = pl.pallas_call(
        functools.partial(_dilp_body, dil=dil),
        grid=(batch, t // sb, nhalf),
        in_specs=[pl.BlockSpec(blk, cur(0)), pl.BlockSpec(blk, cur(1)), pl.BlockSpec(blk, prev(1)),
                  pl.BlockSpec(blk, cur(2)), pl.BlockSpec(blk, prev(2))],
        out_specs=[pl.BlockSpec(blk, outm), pl.BlockSpec(blk, outm)],
        out_shape=[jax.ShapeDtypeStruct((batch, t, SW_W), F32)] * 2,
        compiler_params=_cparams("arbitrary", "arbitrary", "arbitrary"),
        name=f"dilated_prompt_{gi}",
    )(sw3, sw3, sw3, sw3, sw3)
    return o.reshape(batch * t, SW_W), l.reshape(batch * t, SW_W)


def _dils_body(q_ref, kn_ref, vn_ref, buf_ref, o_ref, l_ref, *, win, dil, t_new, wb):
    tq = q_ref.shape[1]
    q = (q_ref[0] * (SW_DH ** -0.5)).astype(BF16)
    kb = buf_ref[0, :, 0:SW_W].astype(BF16)
    vb = buf_ref[0, :, SW_W:2 * SW_W].astype(BF16)

    def stride_mask(nkeys, first_pos):
        ti = lax.broadcasted_iota(jnp.int32, (tq, nkeys), 0)
        ji = first_pos + lax.broadcasted_iota(jnp.int32, (tq, nkeys), 1)
        d = wb + ti - ji
        return (d >= 0) & (d <= win) & ((d & (dil - 1)) == 0) & (ji < wb + t_new)

    o, l = _softmax_heads(q, [(kb, vb, stride_mask(wb, 0)),
                              (kn_ref[0].astype(BF16), vn_ref[0].astype(BF16), stride_mask(tq, wb))])
    o_ref[0] = o
    l_ref[0] = l


def _dilated_sample(q, kn, vn, buf, win, dil, t_new):
    b, tq, _ = q.shape
    wb = buf.shape[1]
    per = lambda i: (i, 0, 0)
    return pl.pallas_call(
        functools.partial(_dils_body, win=win, dil=dil, t_new=t_new, wb=wb),
        grid=(b,),
        in_specs=[pl.BlockSpec((1, tq, SW_W), per)] * 3 + [pl.BlockSpec((1, wb, 2 * SW_W), per)],
        out_specs=[pl.BlockSpec((1, tq, SW_W), per)] * 2,
        out_shape=[jax.ShapeDtypeStruct((b, tq, SW_W), F32)] * 2,
        compiler_params=_cparams("arbitrary"),
        name=f"dilated_sample_{win}",
    )(q, kn, vn, buf)


def _outproj_body(x_ref, odn_ref, o1, o2, o3, l1, l2, l3, wout_ref, g_ref, wmq_ref, h_ref, qm_ref):
    la, lb, lc = l1[...], l2[...], l3[...]
    m = jnp.maximum(jnp.maximum(la, lb), lc)
    ea, eb, ec = jnp.exp(la - m), jnp.exp(lb - m), jnp.exp(lc - m)
    osw = (ea * o1[...] + eb * o2[...] + ec * o3[...]) / (ea + eb + ec)
    mix = _dot(odn_ref[...].astype(BF16), wout_ref[0:DN_V, :]) + _dot(osw.astype(BF16), wout_ref[DN_V:, :])
    h = x_ref[...] + mix
    h_ref[...] = h
    c = _rms(h, g_ref[...]).astype(BF16)
    qm_ref[...] = (_dot(c, wmq_ref[...]) * (MEM_DH ** -0.5)).astype(BF16)


def _out_proj(x, odn, os_, ls_, wout, g_memq, wmq, tm):
    n = x.shape[0]
    row = lambda i: (i, 0)
    fix = lambda i: (0, 0)
    sws = pl.BlockSpec((tm, SW_W), row)
    return pl.pallas_call(
        _outproj_body,
        grid=(n // tm,),
        in_specs=[pl.BlockSpec((tm, D_MODEL), row), pl.BlockSpec((tm, DN_V), row)] + [sws] * 6 +
                 [pl.BlockSpec((DN_V + SW_W, D_MODEL), fix), pl.BlockSpec((1, D_MODEL), fix),
                  pl.BlockSpec((D_MODEL, D_MODEL), fix)],
        out_specs=[pl.BlockSpec((tm, D_MODEL), row), pl.BlockSpec((tm, D_MODEL), row)],
        out_shape=[jax.ShapeDtypeStruct((n, D_MODEL), F32), jax.ShapeDtypeStruct((n, D_MODEL), BF16)],
        compiler_params=_cparams("arbitrary"),
        name="out_proj",
    )(x, odn, *os_, *ls_, wout, g_memq, wmq)


def _memkv_body(m_ref, g_ref, w_ref, kv_ref):
    kv_ref[...] = _dot(_rms(m_ref[...], g_ref[...]).astype(BF16), w_ref[...])


def _memory_kv(mem, g, w, tm=256, tn=1024):
    n = mem.shape[0]
    ncol = w.shape[1]
    return pl.pallas_call(
        _memkv_body,
        grid=(n // tm, ncol // tn),
        in_specs=[pl.BlockSpec((tm, D_MODEL), lambda i, j: (i, 0)), pl.BlockSpec((1, D_MODEL), lambda i, j: (0, 0)),
                  pl.BlockSpec((D_MODEL, tn), lambda i, j: (0, j))],
        out_specs=pl.BlockSpec((tm, tn), lambda i, j: (i, j)),
        out_shape=jax.ShapeDtypeStruct((n, ncol), F32),
        compiler_params=_cparams("arbitrary", "arbitrary"),
        name="memory_kv",
    )(mem, g, w)


def _memattn_body(h_ref, q_ref, k_ref, v_ref, wmo_ref, o_ref):
    q = q_ref[0]
    outs = []
    for hd in range(MEM_HEADS):
        cs = slice(hd * MEM_DH, (hd + 1) * MEM_DH)
        s = _dot_nt(q[:, cs], k_ref[0, :, cs])
        m = jnp.max(s, axis=-1, keepdims=True)
        e = jnp.exp(s - m)
        p = e / jnp.sum(e, axis=-1, keepdims=True)
        outs.append(_dot(p.astype(BF16), v_ref[0, :, cs]))
    o = jnp.concatenate(outs, axis=-1).astype(BF16)
    o_ref[0] = h_ref[0] + _dot(o, wmo_ref[...])


def _mem_attend(h, qm, kvb, wmo, tm):
    b, t, _ = h.shape
    blk = lambda i, j: (i, j, 0)
    return pl.pallas_call(
        _memattn_body,
        grid=(b, t // tm),
        in_specs=[pl.BlockSpec((1, tm, D_MODEL), blk), pl.BlockSpec((1, tm, D_MODEL), blk),
                  pl.BlockSpec((1, MEM_LEN, D_MODEL), lambda i, j: (i, 0, 0)),
                  pl.BlockSpec((1, MEM_LEN, D_MODEL), lambda i, j: (i, 0, 1)),
                  pl.BlockSpec((D_MODEL, D_MODEL), lambda i, j: (0, 0))],
        out_specs=pl.BlockSpec((1, tm, D_MODEL), blk),
        out_shape=jax.ShapeDtypeStruct((b, t, D_MODEL), F32),
        compiler_params=_cparams("arbitrary", "arbitrary"),
        name="mem_attend",
    )(h, qm, kvb, kvb, wmo)


def _topk_rows(s, k, payload=None):
    r = s.shape[0]
    rid = lax.broadcasted_iota(jnp.int32, s.shape, 0).astype(F32)
    vals, picks = [], []
    for _ in range(k):
        m = jnp.max(s, axis=0, keepdims=True)
        idx = jnp.min(jnp.where(s == m, rid, float(r)), axis=0, keepdims=True)
        sel = rid == idx
        vals.append(m)
        if payload is None:
            picks.append(idx)
        else:
            picks.append(jnp.sum(jnp.where(sel, payload, 0.0), axis=0, keepdims=True))
        s = jnp.where(sel, -jnp.inf, s)
    return vals, picks


def _pair_candidates(v1, v2, combine, fill):
    k = PEER_TOPK
    v2all = jnp.concatenate(v2, axis=0)
    bid = lax.broadcasted_iota(jnp.int32, v2all.shape, 0)
    parts = []
    for a in range(4):
        parts.append(jnp.where((a + 1) * (bid + 1) <= k, combine(v1[a], v2all), fill))
    for a in range(4, 8):
        half = combine(v1[a], v2all[0:SUBLANES])
        parts.append(jnp.where((a + 1) * (bid[0:SUBLANES] + 1) <= k, half, fill))
    parts.append(combine(jnp.concatenate(v1[8:16], axis=0), v2[0]))
    return jnp.concatenate(parts, axis=0)


def _route_body(h_ref, g_ref, wh_ref, wl_ref, kh_ref, kl_ref, fhl_ref, eid_ref, gate_ref):
    f = _rms(h_ref[...], g_ref[...])
    fh, fl = _split(f)
    fhl_ref[...] = jnp.concatenate([fh, fl], axis=-1)
    qt = _dot_nt(wh_ref[...], fh) + _dot_nt(wh_ref[...], fl) + _dot_nt(wl_ref[...], fh)
    eids, gates = [], []
    for hd in range(PEER_HEADS):
        sv, si = [], []
        for p in range(2):
            hp = 2 * hd + p
            qh, ql = _split(qt[hp * PEER_DHALF:(hp + 1) * PEER_DHALF, :])
            s = _dot(kh_ref[hp], qh) + _dot(kh_ref[hp], ql) + _dot(kl_ref[hp], qh)
            v, i = _topk_rows(s, PEER_TOPK)
            sv.append(v)
            si.append(i)
        cand = _pair_candidates(sv[0], sv[1], lambda x, y: x + y, -jnp.inf)
        cid = _pair_candidates(si[0], si[1], lambda x, y: x * PEER_NKEYS + y, 0.0)
        cv, ce = _topk_rows(cand, PEER_TOPK, payload=cid)
        cv = jnp.concatenate(cv, axis=0)
        ex = jnp.exp(cv - cv[0:1, :])
        gates.append(ex / jnp.sum(ex, axis=0, keepdims=True))
        eids.append(jnp.concatenate(ce, axis=0) * float(SUBLANES // 2))
    eid_ref[...] = jnp.concatenate(eids, axis=0).T.astype(jnp.int32)
    gate_ref[...] = jnp.concatenate(gates, axis=0).T


def _peer_route(h, g, wpq_h, wpq_l, keys_h, keys_l):
    n = h.shape[0]
    tm = LANES
    row = lambda i: (i, 0)
    fix2 = lambda i: (0, 0)
    fix3 = lambda i: (0, 0, 0)
    nq = PEER_HEADS * PEER_DQ
    return pl.pallas_call(
        _route_body,
        grid=(n // tm,),
        in_specs=[pl.BlockSpec((tm, D_MODEL), row), pl.BlockSpec((1, D_MODEL), fix2),
                  pl.BlockSpec((nq, D_MODEL), fix2), pl.BlockSpec((nq, D_MODEL), fix2),
                  pl.BlockSpec((2 * PEER_HEADS, PEER_NKEYS, PEER_DHALF), fix3),
                  pl.BlockSpec((2 * PEER_HEADS, PEER_NKEYS, PEER_DHALF), fix3)],
        out_specs=[pl.BlockSpec((tm, 2 * D_MODEL), row), pl.BlockSpec((tm, PEER_PAIRS), row),
                   pl.BlockSpec((tm, PEER_PAIRS), row)],
        out_shape=[jax.ShapeDtypeStruct((n, 2 * D_MODEL), BF16), jax.ShapeDtypeStruct((n, PEER_PAIRS), jnp.int32),
                   jax.ShapeDtypeStruct((n, PEER_PAIRS), F32)],
        compiler_params=_cparams("arbitrary"),
        name="peer_route",
    )(h, g, wpq_h, wpq_l, keys_h, keys_l)


def _load_table(tbl_hbm, tbl, sem):
    @pl.when(pl.program_id(0) == 0)
    def _():
        cp = pltpu.make_async_copy(tbl_hbm, tbl, sem)
        cp.start()
        cp.wait()


def _gather_rows(tbl, eid_ref, t, xbuf):
    half = SUBLANES // 2
    tbl32 = tbl.bitcast(jnp.int32)
    xbuf32 = xbuf.bitcast(jnp.int32)
    for k in range(PEER_PAIRS):
        r = pl.multiple_of(eid_ref[t, k], half)
        xbuf32[k * half:(k + 1) * half, :] = tbl32[pl.ds(r, half), :]


def _skewed_token_loop(te, xbufs, gather, compute, unroll=16):
    gather(0, xbufs[0])

    def body(i, carry):
        t = unroll * i
        for j in range(unroll):
            gather(jnp.minimum(t + j + 1, te - 1), xbufs[(j + 1) % 2])
            compute(t + j, xbufs[j % 2])
        return carry

    lax.fori_loop(0, te // unroll, body, 0)


def _chunk_eye():
    r = lax.broadcasted_iota(jnp.int32, (SUBLANES, PEER_PAIRS * SUBLANES), 0)
    c = lax.broadcasted_iota(jnp.int32, (SUBLANES, PEER_PAIRS * SUBLANES), 1)
    return r == (c & (SUBLANES - 1))


def _peer_hid_body(eid_ref, fhl_ref, gate_ref, tbl_hbm, w_ref, tbl, xbuf0, xbuf1, acc, sem, *, te):
    _load_table(tbl_hbm, tbl, sem)
    eye = _chunk_eye()

    def compute(t, xbuf):
        res = _dot_nt(fhl_ref[t], xbuf[...])
        r = res[0:SUBLANES] + res[SUBLANES:2 * SUBLANES]
        acc[pl.ds(t, 1), :] = jnp.sum(jnp.where(eye, r, 0.0), axis=0, keepdims=True)

    _skewed_token_loop(te, (xbuf0, xbuf1), lambda t, xb: _gather_rows(tbl, eid_ref, t, xb), compute)
    a = acc[...]
    a1 = a.astype(BF16)
    a2 = (a - a1.astype(F32)).astype(BF16)
    a3 = (a - a1.astype(F32) - a2.astype(F32)).astype(BF16)
    fr = lax.broadcasted_iota(jnp.int32, (PEER_PAIRS * SUBLANES, PEER_PAIRS), 0) // SUBLANES
    fc = lax.broadcasted_iota(jnp.int32, (PEER_PAIRS * SUBLANES, PEER_PAIRS), 1)
    fold = (fr == fc).astype(BF16)
    hid = _dot(a1, fold) + _dot(a2, fold) + _dot(a3, fold)
    c = math.sqrt(2.0 / math.pi)
    gelu = 0.5 * hid * (1.0 + jnp.tanh(c * (hid + 0.044715 * (hid * hid * hid))))
    w_ref[...] = gate_ref[...] * gelu


def _peer_hid(eid, fhl, gate, tbl, te):
    n = eid.shape[0]
    row = lambda i: (i, 0)
    return pl.pallas_call(
        functools.partial(_peer_hid_body, te=te),
        grid=(n // te,),
        in_specs=[pl.BlockSpec((te, PEER_PAIRS), row, memory_space=pltpu.SMEM),
                  pl.BlockSpec((te, 2 * SUBLANES, LANES), lambda i: (i, 0, 0)),
                  pl.BlockSpec((te, PEER_PAIRS), row),
                  pl.BlockSpec(memory_space=pl.ANY)],
        out_specs=pl.BlockSpec((te, PEER_PAIRS), row),
        out_shape=jax.ShapeDtypeStruct((n, PEER_PAIRS), F32),
        scratch_shapes=[pltpu.VMEM((PEER_N * SUBLANES, LANES), BF16),
                        pltpu.VMEM((PEER_PAIRS * SUBLANES, LANES), BF16),
                        pltpu.VMEM((PEER_PAIRS * SUBLANES, LANES), BF16),
                        pltpu.VMEM((te, PEER_PAIRS * SUBLANES), F32), pltpu.SemaphoreType.DMA],
        compiler_params=_cparams("arbitrary"),
        name="peer_hid",
    )(eid, fhl, gate, tbl)


def _peer_out_body(eid_ref, w_ref, h_ref, tbl_hbm, o_ref, tbl, xbuf0, xbuf1, wexp_hi, wexp_lo, sem, *, te):
    _load_table(tbl_hbm, tbl, sem)
    eye = _chunk_eye()
    w = w_ref[...]
    wh = w.astype(BF16)
    wl = (w - wh.astype(F32)).astype(BF16)
    er = lax.broadcasted_iota(jnp.int32, (PEER_PAIRS, PEER_PAIRS * SUBLANES), 0)
    ec = lax.broadcasted_iota(jnp.int32, (PEER_PAIRS, PEER_PAIRS * SUBLANES), 1) // SUBLANES
    expand = (er == ec).astype(BF16)
    wexp_hi[...] = _dot(wh, expand)
    wexp_lo[...] = _dot(wl, expand)
    wide = (SUBLANES, PEER_PAIRS * SUBLANES)

    def compute(t, xbuf):
        hi = jnp.where(eye, jnp.broadcast_to(wexp_hi[pl.ds(t, 1), :], wide), 0.0)
        lo = jnp.where(eye, jnp.broadcast_to(wexp_lo[pl.ds(t, 1), :], wide), 0.0)
        res = _dot(jnp.concatenate([hi, lo], axis=0).astype(BF16), xbuf[...])
        o_ref[t] = h_ref[t] + (res[0:SUBLANES] + res[SUBLANES:2 * SUBLANES])

    _skewed_token_loop(te, (xbuf0, xbuf1), lambda t, xb: _gather_rows(tbl, eid_ref, t, xb), compute)


def _peer_out(eid, w, h3, tbl, te):
    n = eid.shape[0]
    row = lambda i: (i, 0)
    row3 = lambda i: (i, 0, 0)
    return pl.pallas_call(
        functools.partial(_peer_out_body, te=te),
        grid=(n // te,),
        in_specs=[pl.BlockSpec((te, PEER_PAIRS), row, memory_space=pltpu.SMEM),
                  pl.BlockSpec((te, PEER_PAIRS), row),
                  pl.BlockSpec((te, SUBLANES, LANES), row3),
                  pl.BlockSpec(memory_space=pl.ANY)],
        out_specs=pl.BlockSpec((te, SUBLANES, LANES), row3),
        out_shape=jax.ShapeDtypeStruct((n, SUBLANES, LANES), F32),
        scratch_shapes=[pltpu.VMEM((PEER_N * SUBLANES, LANES), BF16),
                        pltpu.VMEM((PEER_PAIRS * SUBLANES, LANES), BF16),
                        pltpu.VMEM((PEER_PAIRS * SUBLANES, LANES), BF16),
                        pltpu.VMEM((te, PEER_PAIRS * SUBLANES), F32),
                        pltpu.VMEM((te, PEER_PAIRS * SUBLANES), F32), pltpu.SemaphoreType.DMA],
        compiler_params=_cparams("arbitrary"),
        name="peer_out",
    )(eid, w, h3, tbl)


def _final_body(h_ref, g_ref, y_ref):
    y_ref[...] = _rms(h_ref[...], g_ref[...])


def _final_norm(h, g, tm):
    n = h.shape[0]
    return pl.pallas_call(
        _final_body,
        grid=(n // tm,),
        in_specs=[pl.BlockSpec((tm, D_MODEL), lambda i: (i, 0)), pl.BlockSpec((1, D_MODEL), lambda i: (0, 0))],
        out_specs=pl.BlockSpec((tm, D_MODEL), lambda i: (i, 0)),
        out_shape=jax.ShapeDtypeStruct((n, D_MODEL), F32),
        compiler_params=_cparams("arbitrary"),
        name="final_norm",
    )(h, g)


def _pack_table(t):
    return t.astype(BF16).reshape(t.shape[0] * SUBLANES, LANES)


def _prep_weights(g_mix, w_in, conv_w, a_log, dt_bias, g_onorm, w_out, g_memq, g_memkv, w_mq, w_mkv, w_mo,
                  g_ffn, w_pq, sub_keys, expert_u, expert_v, g_final):
    c0 = 2 * DN_QK + DN_V
    gate_cols = w_in[:, c0:c0 + 2 * DN_HEADS]
    z_cols = w_in[:, c0 + 2 * DN_HEADS:c0 + 2 * DN_HEADS + DN_V]
    sw_cols = w_in[:, c0 + 2 * DN_HEADS + DN_V:]
    w_main = jnp.concatenate([w_in[:, :c0], z_cols, sw_cols], axis=1).astype(BF16)
    wg = jnp.pad(gate_cols, ((0, 0), (0, LANES - 2 * DN_HEADS)))
    wgh, wgl = _split(wg)
    wpq_t = w_pq.T
    wpq_h, wpq_l = _split(wpq_t)
    keys = sub_keys.reshape(2 * PEER_HEADS, PEER_NKEYS, PEER_DHALF)
    keys_h, keys_l = _split(keys)
    pad_row = lambda v: jnp.pad(v.astype(F32), (0, LANES - v.shape[0])).reshape(1, LANES)
    return dict(
        g_mix=g_mix.reshape(1, D_MODEL), w_main=w_main, wgh=wgh, wgl=wgl, conv_w=conv_w,
        alog_row=pad_row(a_log), dtb_row=pad_row(dt_bias), g_onorm=g_onorm.reshape(1, DN_DV),
        w_out=w_out.astype(BF16), g_memq=g_memq.reshape(1, D_MODEL), g_memkv=g_memkv.reshape(1, D_MODEL),
        w_mq=w_mq.astype(BF16), w_mkv=w_mkv.astype(BF16), w_mo=w_mo.astype(BF16),
        g_ffn=g_ffn.reshape(1, D_MODEL), wpq_h=wpq_h, wpq_l=wpq_l, keys_h=keys_h, keys_l=keys_l,
        tbl_u=_pack_table(expert_u), tbl_v=_pack_table(expert_v), g_final=g_final.reshape(1, D_MODEL))


def _layer(x, cbuf8, s0, wins, kvb, w, *, chunk, nchunk, t_valid, t_new):
    b, t, _ = x.shape
    n = b * t
    tm = min(256, n)
    x2 = x.reshape(n, D_MODEL)
    qkv, z, sw, gates = _in_proj(x2, w["g_mix"], w["w_main"], w["wgh"], w["wgl"], tm)
    odn, s_new = _deltanet(qkv.reshape(b, t, CONV_DIM), z.reshape(b, t, DN_V), gates.reshape(b, t, LANES),
                           cbuf8, s0, w["conv_w"], w["alog_row"], w["dtb_row"], w["g_onorm"],
                           chunk=chunk, nchunk=nchunk, t_valid=t_valid)
    outs, lses = [], []
    for gi, (win, dil) in enumerate(SW_GROUPS):
        if wins is None:
            o, l = _dilated_prompt(sw, gi, dil, b, t)
        else:
            part = lambda p: sw[:, (3 * gi + p) * SW_W:(3 * gi + p + 1) * SW_W].reshape(b, t, SW_W)
            o, l = _dilated_sample(part(0), part(1), part(2), wins[gi], win, dil, t_new)
            o, l = o.reshape(n, SW_W), l.reshape(n, SW_W)
        outs.append(o)
        lses.append(l)
    h1, qm = _out_proj(x2, odn.reshape(n, DN_V), outs, lses, w["w_out"], w["g_memq"], w["w_mq"], tm)
    tma = min(512, t)
    h2 = _mem_attend(h1.reshape(b, t, D_MODEL), qm.reshape(b, t, D_MODEL), kvb, w["w_mo"], tma)
    h2 = h2.reshape(n, D_MODEL)
    fhl, eid, gate = _peer_route(h2, w["g_ffn"], w["wpq_h"], w["wpq_l"], w["keys_h"], w["keys_l"])
    te = min(128, n)
    wgt = _peer_hid(eid, fhl.reshape(n, 2 * SUBLANES, LANES), gate, w["tbl_u"], te)
    h3 = _peer_out(eid, wgt, h2.reshape(n, SUBLANES, LANES), w["tbl_v"], te)
    y = _final_norm(h3.reshape(n, D_MODEL), w["g_final"], tm)
    return y.reshape(b, t, D_MODEL), qkv.reshape(b, t, CONV_DIM), s_new, sw.reshape(b, t, SW_ALL)


def kernel(x_prompt, x_sample, state_delta, state_conv, cache_win1, cache_win2, cache_win3, cache_mem_kv,
           mem_prompt, g_mix, w_in, conv_w, a_log, dt_bias, g_onorm, w_out, g_memq, g_memkv, w_mq, w_mkv,
           w_mo, g_ffn, w_pq, sub_keys, expert_u, expert_v, g_final):
    w = _prep_weights(g_mix[0], w_in[0], conv_w[0], a_log[0], dt_bias[0], g_onorm[0], w_out[0], g_memq[0],
                      g_memkv[0], w_mq[0], w_mkv[0], w_mo[0], g_ffn[0], w_pq[0], sub_keys[0], expert_u[0],
                      expert_v[0], g_final)
    bp, tp, _ = x_prompt.shape
    bs, ts, _ = x_sample.shape

    kv_p = _memory_kv(mem_prompt.reshape(bp * MEM_LEN, D_MODEL), w["g_memkv"], w["w_mkv"])
    kv_p3 = kv_p.reshape(bp, MEM_LEN, 2 * D_MODEL)
    y_p, qkv_p, sdelta_p, sw_p = _layer(
        x_prompt, jnp.zeros((bp, SUBLANES, CONV_DIM), F32), jnp.zeros((bp, DN_HEADS, DN_DK, DN_DV), F32),
        None, kv_p3.astype(BF16), w, chunk=64, nchunk=2, t_valid=None, t_new=None)
    p_conv = qkv_p[:, tp - (CONV_W - 1):]
    p_wins = []
    for gi, (win, _) in enumerate(SW_GROUPS):
        keep = min(win, tp)
        kvrows = sw_p[:, tp - keep:].reshape(bp, keep, len(SW_GROUPS), 3, SW_HEADS, SW_DH)
        p_wins.append(kvrows[:, :, gi, 1:3])
    p_mem = kv_p.reshape(bp, MEM_LEN, 2, MEM_HEADS, MEM_DH)

    tpad = SUBLANES
    xs = jnp.pad(x_sample, ((0, 0), (0, tpad - ts), (0, 0)))
    cbuf8 = jnp.pad(state_conv[0], ((0, 0), (SUBLANES - (CONV_W - 1), 0), (0, 0)))
    wins = [c[0].reshape(bs, c.shape[2], 2 * SW_W) for c in (cache_win1, cache_win2, cache_win3)]
    kv_s = cache_mem_kv[0].reshape(bs, MEM_LEN, 2 * D_MODEL).astype(BF16)
    y_s, qkv_s, sdelta_s, sw_s = _layer(xs, cbuf8, state_delta[0], wins, kv_s, w,
                                        chunk=SUBLANES, nchunk=1, t_valid=ts, t_new=ts)
    y_s = y_s[:, :ts]
    s_conv = jnp.concatenate([state_conv[0], qkv_s[:, :ts]], axis=1)[:, -(CONV_W - 1):]
    s_wins = []
    for gi, (win, _) in enumerate(SW_GROUPS):
        new = sw_s[:, :ts].reshape(bs, ts, len(SW_GROUPS), 3, SW_HEADS, SW_DH)[:, :, gi, 1:3]
        cache = (cache_win1, cache_win2, cache_win3)[gi][0]
        full = jnp.concatenate([cache, new], axis=1)
        s_wins.append(full[:, -min(win, full.shape[1]):])

    st = lambda a: a[None]
    return (y_p, y_s, st(sdelta_p), st(p_conv), st(p_wins[0]), st(p_wins[1]), st(p_wins[2]), st(p_mem),
            st(sdelta_s), st(s_conv), st(s_wins[0]), st(s_wins[1]), st(s_wins[2]))
```

```python
import functools
import math

import jax
import jax.numpy as jnp
from jax import lax
from jax.experimental import pallas as pl
from jax.experimental.pallas import tpu as pltpu

F32 = jnp.float32
BF16 = jnp.bfloat16

D_MODEL = 1024
DN_HEADS = 4
DN_DK = 128
DN_DV = 128
CONV_W = 4
DN_QK = DN_HEADS * DN_DK
DN_V = DN_HEADS * DN_DV
CONV_DIM = 2 * DN_QK + DN_V
SW_GROUPS = ((128, 1), (512, 4), (2048, 16))
SW_HEADS = 4
SW_DH = 64
SW_W = SW_HEADS * SW_DH
SW_BLOCK = 128
SW_ALL = 3 * len(SW_GROUPS) * SW_W
MEM_LEN = 256
MEM_HEADS = 4
MEM_DH = D_MODEL // MEM_HEADS
PEER_HEADS = 8
PEER_NKEYS = 128
PEER_N = PEER_NKEYS * PEER_NKEYS
PEER_DQ = 256
PEER_DHALF = PEER_DQ // 2
PEER_TOPK = 16
PEER_PAIRS = PEER_HEADS * PEER_TOPK
NORM_EPS = 1e-6
NEG_BIG = -1e30

LANES = 128
SUBLANES = 8
VMEM_LIMIT = 56 * 1024 * 1024


def _cparams(*sem):
    return pltpu.CompilerParams(dimension_semantics=sem, vmem_limit_bytes=VMEM_LIMIT)


def _rms(x, g):
    return x * lax.rsqrt(jnp.mean(x * x, axis=-1, keepdims=True) + NORM_EPS) * g


def _dot(a, b):
    return jnp.dot(a, b, preferred_element_type=F32)


def _dot_nt(a, b):
    return lax.dot_general(a, b, (((1,), (1,)), ((), ())), preferred_element_type=F32)


def _dot_tn(a, b):
    return lax.dot_general(a, b, (((0,), (0,)), ((), ())), preferred_element_type=F32)


def _dot3(a, b, dot=_dot):
    ah, al = _split(a)
    bh, bl = _split(b)
    return dot(ah, bh) + dot(al, bh) + dot(ah, bl)


def _split(x):
    hi = x.astype(BF16)
    lo = (x - hi.astype(F32)).astype(BF16)
    return hi, lo


def _inproj_body(x_ref, g_ref, w_ref, wgh_ref, wgl_ref, qkv_ref, z_ref, sw_ref, gate_ref):
    y = _rms(x_ref[...], g_ref[...])
    yh, yl = _split(y)
    qkv_ref[...] = _dot(yh, w_ref[:, 0:CONV_DIM])
    z_ref[...] = _dot(yh, w_ref[:, CONV_DIM:CONV_DIM + DN_V])
    sw_ref[...] = _dot(yh, w_ref[:, CONV_DIM + DN_V:])
    gate_ref[...] = _dot(yh, wgh_ref[...]) + _dot(yl, wgh_ref[...]) + _dot(yh, wgl_ref[...])


def _in_proj(x, g, w, wgh, wgl, tm):
    n = x.shape[0]
    ncol = w.shape[1]
    row = lambda i: (i, 0)
    fix = lambda i: (0, 0)
    return pl.pallas_call(
        _inproj_body,
        grid=(n // tm,),
        in_specs=[pl.BlockSpec((tm, D_MODEL), row), pl.BlockSpec((1, D_MODEL), fix),
                  pl.BlockSpec((D_MODEL, ncol), fix), pl.BlockSpec((D_MODEL, LANES), fix),
                  pl.BlockSpec((D_MODEL, LANES), fix)],
        out_specs=[pl.BlockSpec((tm, CONV_DIM), row), pl.BlockSpec((tm, DN_V), row),
                   pl.BlockSpec((tm, SW_ALL), row), pl.BlockSpec((tm, LANES), row)],
        out_shape=[jax.ShapeDtypeStruct((n, CONV_DIM), F32), jax.ShapeDtypeStruct((n, DN_V), F32),
                   jax.ShapeDtypeStruct((n, SW_ALL), F32), jax.ShapeDtypeStruct((n, LANES), F32)],
        compiler_params=_cparams("arbitrary"),
        name="in_proj",
    )(x, g, w, wgh, wgl)


def _deltanet_body(qkv_ref, z_ref, gate_ref, cbuf_ref, s0_ref, cw_ref, alog_ref, dtb_ref, gon_ref,
                   o_ref, sfin_ref, state, tail, *, chunk, nchunk, t_valid):
    j = pl.program_id(1)
    tb = chunk * nchunk

    @pl.when(j == 0)
    def _():
        state[...] = s0_ref[0]
        tail[...] = cbuf_ref[0]

    x = qkv_ref[0]
    xp = jnp.concatenate([tail[...], x], axis=0)
    acc = x * cw_ref[CONV_W - 1:CONV_W, :]
    for jj in range(CONV_W - 1):
        sh = pltpu.roll(xp, CONV_W - 1 - jj, 0)[SUBLANES:, :]
        acc = acc + sh * cw_ref[jj:jj + 1, :]
    tail[...] = x[tb - SUBLANES:, :]
    act = acc * jax.nn.sigmoid(acc)

    gt = gate_ref[0]
    xg = gt + dtb_ref[...]
    softplus = jnp.maximum(xg, 0.0) + jnp.log1p(jnp.exp(-jnp.abs(xg)))
    gfull = -jnp.exp(alog_ref[...]) * softplus
    bfull = jax.nn.sigmoid(gt)
    if t_valid is not None:
        rowid = j * tb + lax.broadcasted_iota(jnp.int32, (tb, LANES), 0)
        live = rowid < t_valid
        gfull = jnp.where(live, gfull, 0.0)
        bfull = jnp.where(live, bfull, 0.0)
        act = jnp.where(live[:, 0:1], act, 0.0)

    nh = DN_HEADS
    rows_all = nh * chunk
    ri = lax.broadcasted_iota(jnp.int32, (rows_all, rows_all), 0)
    ci = lax.broadcasted_iota(jnp.int32, (rows_all, rows_all), 1)
    same = (ri // chunk) == (ci // chunk)
    causal = same & (ri >= ci)
    strict = same & (ri > ci)
    eye = (ri == ci).astype(F32)
    ti = lax.broadcasted_iota(jnp.int32, (chunk, chunk), 0)
    tj = lax.broadcasted_iota(jnp.int32, (chunk, chunk), 1)
    ltri = (ti >= tj).astype(F32)
    lane = lax.broadcasted_iota(jnp.int32, (rows_all, LANES), 1)
    hrow = lax.broadcasted_iota(jnp.int32, (rows_all, nh * DN_DK), 0) // chunk
    hcol = lax.broadcasted_iota(jnp.int32, (rows_all, nh * DN_DK), 1) // DN_DK
    own = hrow == hcol
    ndbl = int(math.log2(chunk)) - 1

    def stack(f):
        return jnp.concatenate([f(h) for h in range(nh)], axis=0)

    def widen(a):
        return jnp.where(own, jnp.concatenate([a] * nh, axis=1), 0.0)

    prep = []
    for c in range(nchunk):
        rows = slice(c * chunk, (c + 1) * chunk)
        gc_all = _dot3(ltri, gfull[rows])
        bsl = bfull[rows]
        a_c = act[rows]

        def unit(x):
            return x * lax.rsqrt(jnp.sum(x * x, axis=-1, keepdims=True) + NORM_EPS)

        q = stack(lambda h: unit(a_c[:, h * DN_DK:(h + 1) * DN_DK])) * (DN_DK ** -0.5)
        k = stack(lambda h: unit(a_c[:, DN_QK + h * DN_DK:DN_QK + (h + 1) * DN_DK]))
        v = stack(lambda h: a_c[:, 2 * DN_QK + h * DN_DV:2 * DN_QK + (h + 1) * DN_DV])
        gcb = stack(lambda h: jnp.broadcast_to(gc_all[:, h:h + 1], (chunk, LANES)))
        beta = stack(lambda h: jnp.broadcast_to(bsl[:, nh + h:nh + h + 1], (chunk, LANES)))
        glast = stack(lambda h: jnp.broadcast_to(gc_all[chunk - 1:chunk, h:h + 1], (chunk, LANES)))
        da = jnp.where(lane == 0, gcb, jnp.where(lane == 1, 1.0, 0.0))
        db = jnp.where(lane == 0, 1.0, jnp.where(lane == 1, -gcb, 0.0))
        dmat = _dot3(da, db, _dot_nt)
        gam = jnp.where(causal, jnp.exp(jnp.where(causal, dmat, 0.0)), 0.0)
        eg = jnp.exp(gcb)
        kb = k * beta
        lmat = jnp.where(strict, _dot3(kb, k, _dot_nt) * gam, 0.0)
        inv = eye - lmat
        pw = _dot3(lmat, lmat)
        for it in range(ndbl):
            inv = inv + _dot3(inv, pw)
            if it + 1 < ndbl:
                pw = _dot3(pw, pw)
        sol = _dot3(inv, jnp.concatenate([v * beta, kb * eg], axis=-1))
        aqk = jnp.where(causal, _dot3(q, k, _dot_nt) * gam, 0.0)
        dl = stack(lambda h: jnp.broadcast_to(jnp.exp(gc_all[chunk - 1:chunk, h:h + 1]), (DN_DK, DN_DV)))
        prep.append(dict(u0=sol[:, :DN_DV], wk=widen(sol[:, DN_DV:]), aqk=aqk, qg=widen(q * eg),
                         kd=widen(k * jnp.exp(glast - gcb)), dl=dl))

    for c in range(nchunk):
        rows = slice(c * chunk, (c + 1) * chunk)
        p = prep[c]
        s = state[...]
        u = p["u0"] - _dot3(p["wk"], s)
        o = _dot3(p["qg"], s) + _dot3(p["aqk"], u)
        state[...] = s * p["dl"] + _dot3(p["kd"], u, _dot_tn)
        for h in range(nh):
            zz = z_ref[0, rows, h * DN_DV:(h + 1) * DN_DV]
            o_ref[0, rows, h * DN_DV:(h + 1) * DN_DV] = (
                _rms(o[h * chunk:(h + 1) * chunk], gon_ref[...]) * (zz * jax.nn.sigmoid(zz)))

    @pl.when(j == pl.num_programs(1) - 1)
    def _():
        sfin_ref[0] = state[...]


def _deltanet(qkv, z, gates, cbuf8, s0, conv_w, alog_row, dtb_row, g_onorm, *, chunk, nchunk, t_valid):
    b, t, _ = qkv.shape
    tb = chunk * nchunk
    blk = lambda i, j: (i, j, 0)
    per_b3 = lambda i, j: (i, 0, 0)
    fix = lambda i, j: (0, 0)
    odn, s_new = pl.pallas_call(
        functools.partial(_deltanet_body, chunk=chunk, nchunk=nchunk, t_valid=t_valid),
        grid=(b, t // tb),
        in_specs=[pl.BlockSpec((1, tb, CONV_DIM), blk), pl.BlockSpec((1, tb, DN_V), blk),
                  pl.BlockSpec((1, tb, LANES), blk), pl.BlockSpec((1, SUBLANES, CONV_DIM), per_b3),
                  pl.BlockSpec((1, DN_HEADS * DN_DK, DN_DV), per_b3),
                  pl.BlockSpec((CONV_W, CONV_DIM), fix), pl.BlockSpec((1, LANES), fix),
                  pl.BlockSpec((1, LANES), fix), pl.BlockSpec((1, DN_DV), fix)],
        out_specs=[pl.BlockSpec((1, tb, DN_V), blk), pl.BlockSpec((1, DN_HEADS * DN_DK, DN_DV), per_b3)],
        out_shape=[jax.ShapeDtypeStruct((b, t, DN_V), F32),
                   jax.ShapeDtypeStruct((b, DN_HEADS * DN_DK, DN_DV), F32)],
        scratch_shapes=[pltpu.VMEM((DN_HEADS * DN_DK, DN_DV), F32), pltpu.VMEM((SUBLANES, CONV_DIM), F32)],
        compiler_params=_cparams("arbitrary", "arbitrary"),
        name="deltanet",
    )(qkv, z, gates, cbuf8, s0.reshape(b, DN_HEADS * DN_DK, DN_DV), conv_w, alog_row, dtb_row, g_onorm)
    return odn, s_new.reshape(b, DN_HEADS, DN_DK, DN_DV)


def _softmax_heads(q, segs):
    outs, lses = [], []
    for h in range(q.shape[1] // SW_DH):
        cs = slice(h * SW_DH, (h + 1) * SW_DH)
        ss = [jnp.where(mk, _dot_nt(q[:, cs], k[:, cs]), NEG_BIG) for k, _, mk in segs]
        m = functools.reduce(jnp.maximum, [jnp.max(s, axis=-1, keepdims=True) for s in ss])
        es = [jnp.exp(s - m) for s in ss]
        l = functools.reduce(jnp.add, [jnp.sum(e, axis=-1, keepdims=True) for e in es])
        o = functools.reduce(jnp.add, [_dot((e / l).astype(BF16), v[:, cs]) for e, (_, v, _) in zip(es, segs)])
        outs.append(o)
        lses.append(jnp.broadcast_to(m + jnp.log(l), (q.shape[0], SW_DH)))
    return jnp.concatenate(outs, axis=-1), jnp.concatenate(lses, axis=-1)


def _dilp_body(q_ref, kc_ref, kp_ref, vc_ref, vp_ref, o_ref, l_ref, *, dil):
    n = pl.program_id(1)
    qi = lax.broadcasted_iota(jnp.int32, (SW_BLOCK, SW_BLOCK), 0)
    ki = lax.broadcasted_iota(jnp.int32, (SW_BLOCK, SW_BLOCK), 1)
    mprev = (ki >= qi) & (n > 0)
    mcur = ki <= qi

    def residue(r, carry):
        rows = pl.ds(r, SW_BLOCK, stride=dil) if dil > 1 else slice(None)
        q = (q_ref[0, rows, :] * (SW_DH ** -0.5)).astype(BF16)
        o, l = _softmax_heads(q, [(kp_ref[0, rows, :].astype(BF16), vp_ref[0, rows, :].astype(BF16), mprev),
                                  (kc_ref[0, rows, :].astype(BF16), vc_ref[0, rows, :].astype(BF16), mcur)])
        o_ref[0, rows, :] = o
        l_ref[0, rows, :] = l
        return carry

    if dil > 1:
        lax.fori_loop(0, dil, residue, 0)
    else:
        residue(0, 0)


def _dilated_prompt(sw, gi, dil, batch, t):
    sb = SW_BLOCK * dil
    sw3 = sw.reshape(batch, t, SW_ALL)
    nhalf = SW_W // LANES
    cur = lambda part: (lambda b, n, hf: (b, n, nhalf * (3 * gi + part) + hf))
    prev = lambda part: (lambda b, n, hf: (b, jnp.maximum(n - 1, 0), nhalf * (3 * gi + part) + hf))
    blk = (1, sb, LANES)
    outm = lambda b, n, hf: (b, n, hf)
    o, l = pl.pallas_call(
        functools.partial(_dilp_body, dil=dil),
        grid=(batch, t // sb, nhalf),
        in_specs=[pl.BlockSpec(blk, cur(0)), pl.BlockSpec(blk, cur(1)), pl.BlockSpec(blk, prev(1)),
                  pl.BlockSpec(blk, cur(2)), pl.BlockSpec(blk, prev(2))],
        out_specs=[pl.BlockSpec(blk, outm), pl.BlockSpec(blk, outm)],
        out_shape=[jax.ShapeDtypeStruct((batch, t, SW_W), F32)] * 2,
        compiler_params=_cparams("arbitrary", "arbitrary", "arbitrary"),
        name=f"dilated_prompt_{gi}",
    )(sw3, sw3, sw3, sw3, sw3)
    return o.reshape(batch * t, SW_W), l.reshape(batch * t, SW_W)


def _dils_body(q_ref, kn_ref, vn_ref, buf_ref, o_ref, l_ref, *, win, dil, t_new, wb):
    tq = q_ref.shape[1]
    q = (q_ref[0] * (SW_DH ** -0.5)).astype(BF16)
    kb = buf_ref[0, :, 0:SW_W].astype(BF16)
    vb = buf_ref[0, :, SW_W:2 * SW_W].astype(BF16)

    def stride_mask(nkeys, first_pos):
        ti = lax.broadcasted_iota(jnp.int32, (tq, nkeys), 0)
        ji = first_pos + lax.broadcasted_iota(jnp.int32, (tq, nkeys), 1)
        d = wb + ti - ji
        return (d >= 0) & (d <= win) & ((d & (dil - 1)) == 0) & (ji < wb + t_new)

    o, l = _softmax_heads(q, [(kb, vb, stride_mask(wb, 0)),
                              (kn_ref[0].astype(BF16), vn_ref[0].astype(BF16), stride_mask(tq, wb))])
    o_ref[0] = o
    l_ref[0] = l


def _dilated_sample(q, kn, vn, buf, win, dil, t_new):
    b, tq, _ = q.shape
    wb = buf.shape[1]
    per = lambda i: (i, 0, 0)
    return pl.pallas_call(
        functools.partial(_dils_body, win=win, dil=dil, t_new=t_new, wb=wb),
        grid=(b,),
        in_specs=[pl.BlockSpec((1, tq, SW_W), per)] * 3 + [pl.BlockSpec((1, wb, 2 * SW_W), per)],
        out_specs=[pl.BlockSpec((1, tq, SW_W), per)] * 2,
        out_shape=[jax.ShapeDtypeStruct((b, tq, SW_W), F32)] * 2,
        compiler_params=_cparams("arbitrary"),
        name=f"dilated_sample_{win}",
    )(q, kn, vn, buf)


def _outproj_body(x_ref, odn_ref, o1, o2, o3, l1, l2, l3, wout_ref, g_ref, wmq_ref, h_ref, qm_ref):
    la, lb, lc = l1[...], l2[...], l3[...]
    m = jnp.maximum(jnp.maximum(la, lb), lc)
    ea, eb, ec = jnp.exp(la - m), jnp.exp(lb - m), jnp.exp(lc - m)
    osw = (ea * o1[...] + eb * o2[...] + ec * o3[...]) / (ea + eb + ec)
    mix = _dot(odn_ref[...].astype(BF16), wout_ref[0:DN_V, :]) + _dot(osw.astype(BF16), wout_ref[DN_V:, :])
    h = x_ref[...] + mix
    h_ref[...] = h
    c = _rms(h, g_ref[...]).astype(BF16)
    qm_ref[...] = (_dot(c, wmq_ref[...]) * (MEM_DH ** -0.5)).astype(BF16)


def _out_proj(x, odn, os_, ls_, wout, g_memq, wmq, tm):
    n = x.shape[0]
    row = lambda i: (i, 0)
    fix = lambda i: (0, 0)
    sws = pl.BlockSpec((tm, SW_W), row)
    return pl.pallas_call(
        _outproj_body,
        grid=(n // tm,),
        in_specs=[pl.BlockSpec((tm, D_MODEL), row), pl.BlockSpec((tm, DN_V), row)] + [sws] * 6 +
                 [pl.BlockSpec((DN_V + SW_W, D_MODEL), fix), pl.BlockSpec((1, D_MODEL), fix),
                  pl.BlockSpec((D_MODEL, D_MODEL), fix)],
        out_specs=[pl.BlockSpec((tm, D_MODEL), row), pl.BlockSpec((tm, D_MODEL), row)],
        out_shape=[jax.ShapeDtypeStruct((n, D_MODEL), F32), jax.ShapeDtypeStruct((n, D_MODEL), BF16)],
        compiler_params=_cparams("arbitrary"),
        name="out_proj",
    )(x, odn, *os_, *ls_, wout, g_memq, wmq)


def _memkv_body(m_ref, g_ref, w_ref, kv_ref):
    kv_ref[...] = _dot(_rms(m_ref[...], g_ref[...]).astype(BF16), w_ref[...])


def _memory_kv(mem, g, w, tm=256, tn=1024):
    n = mem.shape[0]
    ncol = w.shape[1]
    return pl.pallas_call(
        _memkv_body,
        grid=(n // tm, ncol // tn),
        in_specs=[pl.BlockSpec((tm, D_MODEL), lambda i, j: (i, 0)), pl.BlockSpec((1, D_MODEL), lambda i, j: (0, 0)),
                  pl.BlockSpec((D_MODEL, tn), lambda i, j: (0, j))],
        out_specs=pl.BlockSpec((tm, tn), lambda i, j: (i, j)),
        out_shape=jax.ShapeDtypeStruct((n, ncol), F32),
        compiler_params=_cparams("arbitrary", "arbitrary"),
        name="memory_kv",
    )(mem, g, w)


def _memattn_body(h_ref, q_ref, k_ref, v_ref, wmo_ref, o_ref):
    q = q_ref[0]
    outs = []
    for hd in range(MEM_HEADS):
        cs = slice(hd * MEM_DH, (hd + 1) * MEM_DH)
        s = _dot_nt(q[:, cs], k_ref[0, :, cs])
        m = jnp.max(s, axis=-1, keepdims=True)
        e = jnp.exp(s - m)
        p = e / jnp.sum(e, axis=-1, keepdims=True)
        outs.append(_dot(p.astype(BF16), v_ref[0, :, cs]))
    o = jnp.concatenate(outs, axis=-1).astype(BF16)
    o_ref[0] = h_ref[0] + _dot(o, wmo_ref[...])


def _mem_attend(h, qm, kvb, wmo, tm):
    b, t, _ = h.shape
    blk = lambda i, j: (i, j, 0)
    return pl.pallas_call(
        _memattn_body,
        grid=(b, t // tm),
        in_specs=[pl.BlockSpec((1, tm, D_MODEL), blk), pl.BlockSpec((1, tm, D_MODEL), blk),
                  pl.BlockSpec((1, MEM_LEN, D_MODEL), lambda i, j: (i, 0, 0)),
                  pl.BlockSpec((1, MEM_LEN, D_MODEL), lambda i, j: (i, 0, 1)),
                  pl.BlockSpec((D_MODEL, D_MODEL), lambda i, j: (0, 0))],
        out_specs=pl.BlockSpec((1, tm, D_MODEL), blk),
        out_shape=jax.ShapeDtypeStruct((b, t, D_MODEL), F32),
        compiler_params=_cparams("arbitrary", "arbitrary"),
        name="mem_attend",
    )(h, qm, kvb, kvb, wmo)


def _topk_rows(s, k, payload=None):
    r = s.shape[0]
    rid = lax.broadcasted_iota(jnp.int32, s.shape, 0).astype(F32)
    vals, picks = [], []
    for _ in range(k):
        m = jnp.max(s, axis=0, keepdims=True)
        idx = jnp.min(jnp.where(s == m, rid, float(r)), axis=0, keepdims=True)
        sel = rid == idx
        vals.append(m)
        if payload is None:
            picks.append(idx)
        else:
            picks.append(jnp.sum(jnp.where(sel, payload, 0.0), axis=0, keepdims=True))
        s = jnp.where(sel, -jnp.inf, s)
    return vals, picks


def _pair_candidates(v1, v2, combine, fill):
    k = PEER_TOPK
    v2all = jnp.concatenate(v2, axis=0)
    bid = lax.broadcasted_iota(jnp.int32, v2all.shape, 0)
    parts = []
    for a in range(4):
        parts.append(jnp.where((a + 1) * (bid + 1) <= k, combine(v1[a], v2all), fill))
    for a in range(4, 8):
        half = combine(v1[a], v2all[0:SUBLANES])
        parts.append(jnp.where((a + 1) * (bid[0:SUBLANES] + 1) <= k, half, fill))
    parts.append(combine(jnp.concatenate(v1[8:16], axis=0), v2[0]))
    return jnp.concatenate(parts, axis=0)


def _route_body(h_ref, g_ref, wh_ref, wl_ref, kh_ref, kl_ref, fhl_ref, eid_ref, gate_ref):
    f = _rms(h_ref[...], g_ref[...])
    fh, fl = _split(f)
    fhl_ref[...] = jnp.concatenate([fh, fl], axis=-1)
    qt = _dot_nt(wh_ref[...], fh) + _dot_nt(wh_ref[...], fl) + _dot_nt(wl_ref[...], fh)
    eids, gates = [], []
    for hd in range(PEER_HEADS):
        sv, si = [], []
        for p in range(2):
            hp = 2 * hd + p
            qh, ql = _split(qt[hp * PEER_DHALF:(hp + 1) * PEER_DHALF, :])
            s = _dot(kh_ref[hp], qh) + _dot(kh_ref[hp], ql) + _dot(kl_ref[hp], qh)
            v, i = _topk_rows(s, PEER_TOPK)
            sv.append(v)
            si.append(i)
        cand = _pair_candidates(sv[0], sv[1], lambda x, y: x + y, -jnp.inf)
        cid = _pair_candidates(si[0], si[1], lambda x, y: x * PEER_NKEYS + y, 0.0)
        cv, ce = _topk_rows(cand, PEER_TOPK, payload=cid)
        cv = jnp.concatenate(cv, axis=0)
        ex = jnp.exp(cv - cv[0:1, :])
        gates.append(ex / jnp.sum(ex, axis=0, keepdims=True))
        eids.append(jnp.concatenate(ce, axis=0))
    eid_ref[...] = jnp.concatenate(eids, axis=0).T.astype(jnp.int32)
    gate_ref[...] = jnp.concatenate(gates, axis=0).T


def _peer_route(h, g, wpq_h, wpq_l, keys_h, keys_l):
    n = h.shape[0]
    tm = LANES
    row = lambda i: (i, 0)
    fix2 = lambda i: (0, 0)
    fix3 = lambda i: (0, 0, 0)
    nq = PEER_HEADS * PEER_DQ
    return pl.pallas_call(
        _route_body,
        grid=(n // tm,),
        in_specs=[pl.BlockSpec((tm, D_MODEL), row), pl.BlockSpec((1, D_MODEL), fix2),
                  pl.BlockSpec((nq, D_MODEL), fix2), pl.BlockSpec((nq, D_MODEL), fix2),
                  pl.BlockSpec((2 * PEER_HEADS, PEER_NKEYS, PEER_DHALF), fix3),
                  pl.BlockSpec((2 * PEER_HEADS, PEER_NKEYS, PEER_DHALF), fix3)],
        out_specs=[pl.BlockSpec((tm, 2 * D_MODEL), row), pl.BlockSpec((tm, PEER_PAIRS), row),
                   pl.BlockSpec((tm, PEER_PAIRS), row)],
        out_shape=[jax.ShapeDtypeStruct((n, 2 * D_MODEL), BF16), jax.ShapeDtypeStruct((n, PEER_PAIRS), jnp.int32),
                   jax.ShapeDtypeStruct((n, PEER_PAIRS), F32)],
        compiler_params=_cparams("arbitrary"),
        name="peer_route",
    )(h, g, wpq_h, wpq_l, keys_h, keys_l)


def _load_table(tbl_hbm, tbl, sem):
    @pl.when(pl.program_id(0) == 0)
    def _():
        cp = pltpu.make_async_copy(tbl_hbm, tbl, sem)
        cp.start()
        cp.wait()


def _gather_rows(tbl, eid_ref, t, xbuf):
    for k in range(PEER_PAIRS):
        xbuf[k * SUBLANES:(k + 1) * SUBLANES, :] = tbl[eid_ref[t, k]]


def _skewed_token_loop(te, xbufs, gather, compute, unroll=16):
    gather(0, xbufs[0])

    def body(i, carry):
        t = unroll * i
        for j in range(unroll):
            gather(jnp.minimum(t + j + 1, te - 1), xbufs[(j + 1) % 2])
            compute(t + j, xbufs[j % 2])
        return carry

    lax.fori_loop(0, te // unroll, body, 0)


def _chunk_eye():
    r = lax.broadcasted_iota(jnp.int32, (SUBLANES, PEER_PAIRS * SUBLANES), 0)
    c = lax.broadcasted_iota(jnp.int32, (SUBLANES, PEER_PAIRS * SUBLANES), 1)
    return r == (c & (SUBLANES - 1))


def _peer_hid_body(eid_ref, fhl_ref, gate_ref, tbl_hbm, w_ref, tbl, xbuf0, xbuf1, acc, sem, *, te):
    _load_table(tbl_hbm, tbl, sem)
    eye = _chunk_eye()

    def compute(t, xbuf):
        res = _dot_nt(fhl_ref[t], xbuf[...])
        r = res[0:SUBLANES] + res[SUBLANES:2 * SUBLANES]
        acc[pl.ds(t, 1), :] = jnp.sum(jnp.where(eye, r, 0.0), axis=0, keepdims=True)

    _skewed_token_loop(te, (xbuf0, xbuf1), lambda t, xb: _gather_rows(tbl, eid_ref, t, xb), compute)
    a = acc[...]
    a1 = a.astype(BF16)
    a2 = (a - a1.astype(F32)).astype(BF16)
    a3 = (a - a1.astype(F32) - a2.astype(F32)).astype(BF16)
    fr = lax.broadcasted_iota(jnp.int32, (PEER_PAIRS * SUBLANES, PEER_PAIRS), 0) // SUBLANES
    fc = lax.broadcasted_iota(jnp.int32, (PEER_PAIRS * SUBLANES, PEER_PAIRS), 1)
    fold = (fr == fc).astype(BF16)
    hid = _dot(a1, fold) + _dot(a2, fold) + _dot(a3, fold)
    c = math.sqrt(2.0 / math.pi)
    gelu = 0.5 * hid * (1.0 + jnp.tanh(c * (hid + 0.044715 * (hid * hid * hid))))
    w_ref[...] = gate_ref[...] * gelu


def _peer_hid(eid, fhl, gate, tbl, te):
    n = eid.shape[0]
    row = lambda i: (i, 0)
    return pl.pallas_call(
        functools.partial(_peer_hid_body, te=te),
        grid=(n // te,),
        in_specs=[pl.BlockSpec((te, PEER_PAIRS), row, memory_space=pltpu.SMEM),
                  pl.BlockSpec((te, 2 * SUBLANES, LANES), lambda i: (i, 0, 0)),
                  pl.BlockSpec((te, PEER_PAIRS), row),
                  pl.BlockSpec(memory_space=pl.ANY)],
        out_specs=pl.BlockSpec((te, PEER_PAIRS), row),
        out_shape=jax.ShapeDtypeStruct((n, PEER_PAIRS), F32),
        scratch_shapes=[pltpu.VMEM((PEER_N, SUBLANES, LANES), BF16),
                        pltpu.VMEM((PEER_PAIRS * SUBLANES, LANES), BF16),
                        pltpu.VMEM((PEER_PAIRS * SUBLANES, LANES), BF16),
                        pltpu.VMEM((te, PEER_PAIRS * SUBLANES), F32), pltpu.SemaphoreType.DMA],
        compiler_params=_cparams("arbitrary"),
        name="peer_hid",
    )(eid, fhl, gate, tbl)


def _peer_out_body(eid_ref, w_ref, h_ref, tbl_hbm, o_ref, tbl, xbuf0, xbuf1, wexp_hi, wexp_lo, sem, *, te):
    _load_table(tbl_hbm, tbl, sem)
    eye = _chunk_eye()
    w = w_ref[...]
    wh = w.astype(BF16)
    wl = (w - wh.astype(F32)).astype(BF16)
    er = lax.broadcasted_iota(jnp.int32, (PEER_PAIRS, PEER_PAIRS * SUBLANES), 0)
    ec = lax.broadcasted_iota(jnp.int32, (PEER_PAIRS, PEER_PAIRS * SUBLANES), 1) // SUBLANES
    expand = (er == ec).astype(BF16)
    wexp_hi[...] = _dot(wh, expand)
    wexp_lo[...] = _dot(wl, expand)
    wide = (SUBLANES, PEER_PAIRS * SUBLANES)

    def compute(t, xbuf):
        hi = jnp.where(eye, jnp.broadcast_to(wexp_hi[pl.ds(t, 1), :], wide), 0.0)
        lo = jnp.where(eye, jnp.broadcast_to(wexp_lo[pl.ds(t, 1), :], wide), 0.0)
        res = _dot(jnp.concatenate([hi, lo], axis=0).astype(BF16), xbuf[...])
        o_ref[t] = h_ref[t] + (res[0:SUBLANES] + res[SUBLANES:2 * SUBLANES])

    _skewed_token_loop(te, (xbuf0, xbuf1), lambda t, xb: _gather_rows(tbl, eid_ref, t, xb), compute)


def _peer_out(eid, w, h3, tbl, te):
    n = eid.shape[0]
    row = lambda i: (i, 0)
    row3 = lambda i: (i, 0, 0)
    return pl.pallas_call(
        functools.partial(_peer_out_body, te=te),
        grid=(n // te,),
        in_specs=[pl.BlockSpec((te, PEER_PAIRS), row, memory_space=pltpu.SMEM),
                  pl.BlockSpec((te, PEER_PAIRS), row),
                  pl.BlockSpec((te, SUBLANES, LANES), row3),
                  pl.BlockSpec(memory_space=pl.ANY)],
        out_specs=pl.BlockSpec((te, SUBLANES, LANES), row3),
        out_shape=jax.ShapeDtypeStruct((n, SUBLANES, LANES), F32),
        scratch_shapes=[pltpu.VMEM((PEER_N, SUBLANES, LANES), BF16),
                        pltpu.VMEM((PEER_PAIRS * SUBLANES, LANES), BF16),
                        pltpu.VMEM((PEER_PAIRS * SUBLANES, LANES), BF16),
                        pltpu.VMEM((te, PEER_PAIRS * SUBLANES), F32),
                        pltpu.VMEM((te, PEER_PAIRS * SUBLANES), F32), pltpu.SemaphoreType.DMA],
        compiler_params=_cparams("arbitrary"),
        name="peer_out",
    )(eid, w, h3, tbl)


def _final_body(h_ref, g_ref, y_ref):
    y_ref[...] = _rms(h_ref[...], g_ref[...])


def _final_norm(h, g, tm):
    n = h.shape[0]
    return pl.pallas_call(
        _final_body,
        grid=(n // tm,),
        in_specs=[pl.BlockSpec((tm, D_MODEL), lambda i: (i, 0)), pl.BlockSpec((1, D_MODEL), lambda i: (0, 0))],
        out_specs=pl.BlockSpec((tm, D_MODEL), lambda i: (i, 0)),
        out_shape=jax.ShapeDtypeStruct((n, D_MODEL), F32),
        compiler_params=_cparams("arbitrary"),
        name="final_norm",
    )(h, g)


def _pack_table(t):
    return t.astype(BF16).reshape(t.shape[0], SUBLANES, LANES)


def _prep_weights(g_mix, w_in, conv_w, a_log, dt_bias, g_onorm, w_out, g_memq, g_memkv, w_mq, w_mkv, w_mo,
                  g_ffn, w_pq, sub_keys, expert_u, expert_v, g_final):
    c0 = 2 * DN_QK + DN_V
    gate_cols = w_in[:, c0:c0 + 2 * DN_HEADS]
    z_cols = w_in[:, c0 + 2 * DN_HEADS:c0 + 2 * DN_HEADS + DN_V]
    sw_cols = w_in[:, c0 + 2 * DN_HEADS + DN_V:]
    w_main = jnp.concatenate([w_in[:, :c0], z_cols, sw_cols], axis=1).astype(BF16)
    wg = jnp.pad(gate_cols, ((0, 0), (0, LANES - 2 * DN_HEADS)))
    wgh, wgl = _split(wg)
    wpq_t = w_pq.T
    wpq_h, wpq_l = _split(wpq_t)
    keys = sub_keys.reshape(2 * PEER_HEADS, PEER_NKEYS, PEER_DHALF)
    keys_h, keys_l = _split(keys)
    pad_row = lambda v: jnp.pad(v.astype(F32), (0, LANES - v.shape[0])).reshape(1, LANES)
    return dict(
        g_mix=g_mix.reshape(1, D_MODEL), w_main=w_main, wgh=wgh, wgl=wgl, conv_w=conv_w,
        alog_row=pad_row(a_log), dtb_row=pad_row(dt_bias), g_onorm=g_onorm.reshape(1, DN_DV),
        w_out=w_out.astype(BF16), g_memq=g_memq.reshape(1, D_MODEL), g_memkv=g_memkv.reshape(1, D_MODEL),
        w_mq=w_mq.astype(BF16), w_mkv=w_mkv.astype(BF16), w_mo=w_mo.astype(BF16),
        g_ffn=g_ffn.reshape(1, D_MODEL), wpq_h=wpq_h, wpq_l=wpq_l, keys_h=keys_h, keys_l=keys_l,
        tbl_u=_pack_table(expert_u), tbl_v=_pack_table(expert_v), g_final=g_final.reshape(1, D_MODEL))


def _layer(x, cbuf8, s0, wins, kvb, w, *, chunk, nchunk, t_valid, t_new):
    b, t, _ = x.shape
    n = b * t
    tm = min(256, n)
    x2 = x.reshape(n, D_MODEL)
    qkv, z, sw, gates = _in_proj(x2, w["g_mix"], w["w_main"], w["wgh"], w["wgl"], tm)
    odn, s_new = _deltanet(qkv.reshape(b, t, CONV_DIM), z.reshape(b, t, DN_V), gates.reshape(b, t, LANES),
                           cbuf8, s0, w["conv_w"], w["alog_row"], w["dtb_row"], w["g_onorm"],
                           chunk=chunk, nchunk=nchunk, t_valid=t_valid)
    outs, lses = [], []
    for gi, (win, dil) in enumerate(SW_GROUPS):
        if wins is None:
            o, l = _dilated_prompt(sw, gi, dil, b, t)
        else:
            part = lambda p: sw[:, (3 * gi + p) * SW_W:(3 * gi + p + 1) * SW_W].reshape(b, t, SW_W)
            o, l = _dilated_sample(part(0), part(1), part(2), wins[gi], win, dil, t_new)
            o, l = o.reshape(n, SW_W), l.reshape(n, SW_W)
        outs.append(o)
        lses.append(l)
    h1, qm = _out_proj(x2, odn.reshape(n, DN_V), outs, lses, w["w_out"], w["g_memq"], w["w_mq"], tm)
    tma = min(512, t)
    h2 = _mem_attend(h1.reshape(b, t, D_MODEL), qm.reshape(b, t, D_MODEL), kvb, w["w_mo"], tma)
    h2 = h2.reshape(n, D_MODEL)
    fhl, eid, gate = _peer_route(h2, w["g_ffn"], w["wpq_h"], w["wpq_l"], w["keys_h"], w["keys_l"])
    te = min(128, n)
    wgt = _peer_hid(eid, fhl.reshape(n, 2 * SUBLANES, LANES), gate, w["tbl_u"], te)
    h3 = _peer_out(eid, wgt, h2.reshape(n, SUBLANES, LANES), w["tbl_v"], te)
    y = _final_norm(h3.reshape(n, D_MODEL), w["g_final"], tm)
    return y.reshape(b, t, D_MODEL), qkv.reshape(b, t, CONV_DIM), s_new, sw.reshape(b, t, SW_ALL)


def kernel(x_prompt, x_sample, state_delta, state_conv, cache_win1, cache_win2, cache_win3, cache_mem_kv,
           mem_prompt, g_mix, w_in, conv_w, a_log, dt_bias, g_onorm, w_out, g_memq, g_memkv, w_mq, w_mkv,
           w_mo, g_ffn, w_pq, sub_keys, expert_u, expert_v, g_final):
    w = _prep_weights(g_mix[0], w_in[0], conv_w[0], a_log[0], dt_bias[0], g_onorm[0], w_out[0], g_memq[0],
                      g_memkv[0], w_mq[0], w_mkv[0], w_mo[0], g_ffn[0], w_pq[0], sub_keys[0], expert_u[0],
                      expert_v[0], g_final)
    bp, tp, _ = x_prompt.shape
    bs, ts, _ = x_sample.shape

    kv_p = _memory_kv(mem_prompt.reshape(bp * MEM_LEN, D_MODEL), w["g_memkv"], w["w_mkv"])
    kv_p3 = kv_p.reshape(bp, MEM_LEN, 2 * D_MODEL)
    y_p, qkv_p, sdelta_p, sw_p = _layer(
        x_prompt, jnp.zeros((bp, SUBLANES, CONV_DIM), F32), jnp.zeros((bp, DN_HEADS, DN_DK, DN_DV), F32),
        None, kv_p3.astype(BF16), w, chunk=64, nchunk=2, t_valid=None, t_new=None)
    p_conv = qkv_p[:, tp - (CONV_W - 1):]
    p_wins = []
    for gi, (win, _) in enumerate(SW_GROUPS):
        keep = min(win, tp)
        kvrows = sw_p[:, tp - keep:].reshape(bp, keep, len(SW_GROUPS), 3, SW_HEADS, SW_DH)
        p_wins.append(kvrows[:, :, gi, 1:3])
    p_mem = kv_p.reshape(bp, MEM_LEN, 2, MEM_HEADS, MEM_DH)

    tpad = SUBLANES
    xs = jnp.pad(x_sample, ((0, 0), (0, tpad - ts), (0, 0)))
    cbuf8 = jnp.pad(state_conv[0], ((0, 0), (SUBLANES - (CONV_W - 1), 0), (0, 0)))
    wins = [c[0].reshape(bs, c.shape[2], 2 * SW_W) for c in (cache_win1, cache_win2, cache_win3)]
    kv_s = cache_mem_kv[0].reshape(bs, MEM_LEN, 2 * D_MODEL).astype(BF16)
    y_s, qkv_s, sdelta_s, sw_s = _layer(xs, cbuf8, state_delta[0], wins, kv_s, w,
                                        chunk=SUBLANES, nchunk=1, t_valid=ts, t_new=ts)
    y_s = y_s[:, :ts]
    s_conv = jnp.concatenate([state_conv[0], qkv_s[:, :ts]], axis=1)[:, -(CONV_W - 1):]
    s_wins = []
    for gi, (win, _) in enumerate(SW_GROUPS):
        new = sw_s[:, :ts].reshape(bs, ts, len(SW_GROUPS), 3, SW_HEADS, SW_DH)[:, :, gi, 1:3]
        cache = (cache_win1, cache_win2, cache_win3)[gi][0]
        full = jnp.concatenate([cache, new], axis=1)
        s_wins.append(full[:, -min(win, full.shape[1]):])

    st = lambda a: a[None]
    return (y_p, y_s, st(sdelta_p), st(p_conv), st(p_wins[0]), st(p_wins[1]), st(p_wins[2]), st(p_mem),
            st(sdelta_s), st(s_conv), st(s_wins[0]), st(s_wins[1]), st(s_wins[2]))
```

```python
import functools
import math

import jax
import jax.numpy as jnp
from jax import lax
from jax.experimental import pallas as pl
from jax.experimental.pallas import tpu as pltpu

F32 = jnp.float32
BF16 = jnp.bfloat16

D_MODEL = 1024
DN_HEADS = 4
DN_DK = 128
DN_DV = 128
CONV_W = 4
DN_QK = DN_HEADS * DN_DK
DN_V = DN_HEADS * DN_DV
CONV_DIM = 2 * DN_QK + DN_V
SW_GROUPS = ((128, 1), (512, 4), (2048, 16))
SW_HEADS = 4
SW_DH = 64
SW_W = SW_HEADS * SW_DH
SW_BLOCK = 128
SW_ALL = 3 * len(SW_GROUPS) * SW_W
MEM_LEN = 256
MEM_HEADS = 4
MEM_DH = D_MODEL // MEM_HEADS
PEER_HEADS = 8
PEER_NKEYS = 128
PEER_N = PEER_NKEYS * PEER_NKEYS
PEER_DQ = 256
PEER_DHALF = PEER_DQ // 2
PEER_TOPK = 16
PEER_PAIRS = PEER_HEADS * PEER_TOPK
NORM_EPS = 1e-6
NEG_BIG = -1e30

LANES = 128
SUBLANES = 8
VMEM_LIMIT = 56 * 1024 * 1024


def _cparams(*sem):
    return pltpu.CompilerParams(dimension_semantics=sem, vmem_limit_bytes=VMEM_LIMIT)


def _rms(x, g):
    return x * lax.rsqrt(jnp.mean(x * x, axis=-1, keepdims=True) + NORM_EPS) * g


def _dot(a, b):
    return jnp.dot(a, b, preferred_element_type=F32)


def _dot_nt(a, b):
    return lax.dot_general(a, b, (((1,), (1,)), ((), ())), preferred_element_type=F32)


def _dot_tn(a, b):
    return lax.dot_general(a, b, (((0,), (0,)), ((), ())), preferred_element_type=F32)


def _dot3(a, b, dot=_dot):
    ah, al = _split(a)
    bh, bl = _split(b)
    return dot(ah, bh) + dot(al, bh) + dot(ah, bl)


def _split(x):
    hi = x.astype(BF16)
    lo = (x - hi.astype(F32)).astype(BF16)
    return hi, lo


def _inproj_body(x_ref, g_ref, w_ref, wgh_ref, wgl_ref, qkv_ref, z_ref, sw_ref, gate_ref):
    y = _rms(x_ref[...], g_ref[...])
    yh, yl = _split(y)
    qkv_ref[...] = _dot(yh, w_ref[:, 0:CONV_DIM])
    z_ref[...] = _dot(yh, w_ref[:, CONV_DIM:CONV_DIM + DN_V])
    sw_ref[...] = _dot(yh, w_ref[:, CONV_DIM + DN_V:])
    gate_ref[...] = _dot(yh, wgh_ref[...]) + _dot(yl, wgh_ref[...]) + _dot(yh, wgl_ref[...])


def _in_proj(x, g, w, wgh, wgl, tm):
    n = x.shape[0]
    ncol = w.shape[1]
    row = lambda i: (i, 0)
    fix = lambda i: (0, 0)
    return pl.pallas_call(
        _inproj_body,
        grid=(n // tm,),
        in_specs=[pl.BlockSpec((tm, D_MODEL), row), pl.BlockSpec((1, D_MODEL), fix),
                  pl.BlockSpec((D_MODEL, ncol), fix), pl.BlockSpec((D_MODEL, LANES), fix),
                  pl.BlockSpec((D_MODEL, LANES), fix)],
        out_specs=[pl.BlockSpec((tm, CONV_DIM), row), pl.BlockSpec((tm, DN_V), row),
                   pl.BlockSpec((tm, SW_ALL), row), pl.BlockSpec((tm, LANES), row)],
        out_shape=[jax.ShapeDtypeStruct((n, CONV_DIM), F32), jax.ShapeDtypeStruct((n, DN_V), F32),
                   jax.ShapeDtypeStruct((n, SW_ALL), F32), jax.ShapeDtypeStruct((n, LANES), F32)],
        compiler_params=_cparams("arbitrary"),
        name="in_proj",
    )(x, g, w, wgh, wgl)


def _deltanet_body(qkv_ref, z_ref, gate_ref, cbuf_ref, s0_ref, cw_ref, alog_ref, dtb_ref, gon_ref,
                   o_ref, sfin_ref, state, tail, *, chunk, nchunk, t_valid):
    j = pl.program_id(1)
    tb = chunk * nchunk

    @pl.when(j == 0)
    def _():
        state[...] = s0_ref[0]
        tail[...] = cbuf_ref[0]

    x = qkv_ref[0]
    xp = jnp.concatenate([tail[...], x], axis=0)
    acc = x * cw_ref[CONV_W - 1:CONV_W, :]
    for jj in range(CONV_W - 1):
        sh = pltpu.roll(xp, CONV_W - 1 - jj, 0)[SUBLANES:, :]
        acc = acc + sh * cw_ref[jj:jj + 1, :]
    tail[...] = x[tb - SUBLANES:, :]
    act = acc * jax.nn.sigmoid(acc)

    gt = gate_ref[0]
    xg = gt + dtb_ref[...]
    softplus = jnp.maximum(xg, 0.0) + jnp.log1p(jnp.exp(-jnp.abs(xg)))
    gfull = -jnp.exp(alog_ref[...]) * softplus
    bfull = jax.nn.sigmoid(gt)
    if t_valid is not None:
        rowid = j * tb + lax.broadcasted_iota(jnp.int32, (tb, LANES), 0)
        live = rowid < t_valid
        gfull = jnp.where(live, gfull, 0.0)
        bfull = jnp.where(live, bfull, 0.0)
        act = jnp.where(live[:, 0:1], act, 0.0)

    nh = DN_HEADS
    rows_all = nh * chunk
    ri = lax.broadcasted_iota(jnp.int32, (rows_all, rows_all), 0)
    ci = lax.broadcasted_iota(jnp.int32, (rows_all, rows_all), 1)
    same = (ri // chunk) == (ci // chunk)
    causal = same & (ri >= ci)
    strict = same & (ri > ci)
    eye = (ri == ci).astype(F32)
    ti = lax.broadcasted_iota(jnp.int32, (chunk, chunk), 0)
    tj = lax.broadcasted_iota(jnp.int32, (chunk, chunk), 1)
    ltri = (ti >= tj).astype(F32)
    lane = lax.broadcasted_iota(jnp.int32, (rows_all, LANES), 1)
    hrow = lax.broadcasted_iota(jnp.int32, (rows_all, nh * DN_DK), 0) // chunk
    hcol = lax.broadcasted_iota(jnp.int32, (rows_all, nh * DN_DK), 1) // DN_DK
    own = hrow == hcol
    ndbl = int(math.log2(chunk)) - 1

    def stack(f):
        return jnp.concatenate([f(h) for h in range(nh)], axis=0)

    def widen(a):
        return jnp.where(own, jnp.concatenate([a] * nh, axis=1), 0.0)

    prep = []
    for c in range(nchunk):
        rows = slice(c * chunk, (c + 1) * chunk)
        gc_all = _dot3(ltri, gfull[rows])
        bsl = bfull[rows]
        a_c = act[rows]

        def unit(x):
            return x * lax.rsqrt(jnp.sum(x * x, axis=-1, keepdims=True) + NORM_EPS)

        q = stack(lambda h: unit(a_c[:, h * DN_DK:(h + 1) * DN_DK])) * (DN_DK ** -0.5)
        k = stack(lambda h: unit(a_c[:, DN_QK + h * DN_DK:DN_QK + (h + 1) * DN_DK]))
        v = stack(lambda h: a_c[:, 2 * DN_QK + h * DN_DV:2 * DN_QK + (h + 1) * DN_DV])
        gcb = stack(lambda h: jnp.broadcast_to(gc_all[:, h:h + 1], (chunk, LANES)))
        beta = stack(lambda h: jnp.broadcast_to(bsl[:, nh + h:nh + h + 1], (chunk, LANES)))
        glast = stack(lambda h: jnp.broadcast_to(gc_all[chunk - 1:chunk, h:h + 1], (chunk, LANES)))
        da = jnp.where(lane == 0, gcb, jnp.where(lane == 1, 1.0, 0.0))
        db = jnp.where(lane == 0, 1.0, jnp.where(lane == 1, -gcb, 0.0))
        dmat = _dot3(da, db, _dot_nt)
        gam = jnp.where(causal, jnp.exp(jnp.where(causal, dmat, 0.0)), 0.0)
        eg = jnp.exp(gcb)
        kb = k * beta
        lmat = jnp.where(strict, _dot3(kb, k, _dot_nt) * gam, 0.0)
        inv = eye - lmat
        pw = _dot3(lmat, lmat)
        for it in range(ndbl):
            inv = inv + _dot3(inv, pw)
            if it + 1 < ndbl:
                pw = _dot3(pw, pw)
        sol = _dot3(inv, jnp.concatenate([v * beta, kb * eg], axis=-1))
        aqk = jnp.where(causal, _dot3(q, k, _dot_nt) * gam, 0.0)
        dl = stack(lambda h: jnp.broadcast_to(jnp.exp(gc_all[chunk - 1:chunk, h:h + 1]), (DN_DK, DN_DV)))
        prep.append(dict(u0=sol[:, :DN_DV], wk=widen(sol[:, DN_DV:]), aqk=aqk, qg=widen(q * eg),
                         kd=widen(k * jnp.exp(glast - gcb)), dl=dl))

    for c in range(nchunk):
        rows = slice(c * chunk, (c + 1) * chunk)
        p = prep[c]
        s = state[...]
        u = p["u0"] - _dot3(p["wk"], s)
        o = _dot3(p["qg"], s) + _dot3(p["aqk"], u)
        state[...] = s * p["dl"] + _dot3(p["kd"], u, _dot_tn)
        for h in range(nh):
            zz = z_ref[0, rows, h * DN_DV:(h + 1) * DN_DV]
            o_ref[0, rows, h * DN_DV:(h + 1) * DN_DV] = (
                _rms(o[h * chunk:(h + 1) * chunk], gon_ref[...]) * (zz * jax.nn.sigmoid(zz)))

    @pl.when(j == pl.num_programs(1) - 1)
    def _():
        sfin_ref[0] = state[...]


def _deltanet(qkv, z, gates, cbuf8, s0, conv_w, alog_row, dtb_row, g_onorm, *, chunk, nchunk, t_valid):
    b, t, _ = qkv.shape
    tb = chunk * nchunk
    blk = lambda i, j: (i, j, 0)
    per_b3 = lambda i, j: (i, 0, 0)
    fix = lambda i, j: (0, 0)
    odn, s_new = pl.pallas_call(
        functools.partial(_deltanet_body, chunk=chunk, nchunk=nchunk, t_valid=t_valid),
        grid=(b, t // tb),
        in_specs=[pl.BlockSpec((1, tb, CONV_DIM), blk), pl.BlockSpec((1, tb, DN_V), blk),
                  pl.BlockSpec((1, tb, LANES), blk), pl.BlockSpec((1, SUBLANES, CONV_DIM), per_b3),
                  pl.BlockSpec((1, DN_HEADS * DN_DK, DN_DV), per_b3),
                  pl.BlockSpec((CONV_W, CONV_DIM), fix), pl.BlockSpec((1, LANES), fix),
                  pl.BlockSpec((1, LANES), fix), pl.BlockSpec((1, DN_DV), fix)],
        out_specs=[pl.BlockSpec((1, tb, DN_V), blk), pl.BlockSpec((1, DN_HEADS * DN_DK, DN_DV), per_b3)],
        out_shape=[jax.ShapeDtypeStruct((b, t, DN_V), F32),
                   jax.ShapeDtypeStruct((b, DN_HEADS * DN_DK, DN_DV), F32)],
        scratch_shapes=[pltpu.VMEM((DN_HEADS * DN_DK, DN_DV), F32), pltpu.VMEM((SUBLANES, CONV_DIM), F32)],
        compiler_params=_cparams("arbitrary", "arbitrary"),
        name="deltanet",
    )(qkv, z, gates, cbuf8, s0.reshape(b, DN_HEADS * DN_DK, DN_DV), conv_w, alog_row, dtb_row, g_onorm)
    return odn, s_new.reshape(b, DN_HEADS, DN_DK, DN_DV)


def _softmax_heads(q, segs):
    outs, lses = [], []
    for h in range(q.shape[1] // SW_DH):
        cs = slice(h * SW_DH, (h + 1) * SW_DH)
        ss = [jnp.where(mk, _dot_nt(q[:, cs], k[:, cs]), NEG_BIG) for k, _, mk in segs]
        m = functools.reduce(jnp.maximum, [jnp.max(s, axis=-1, keepdims=True) for s in ss])
        es = [jnp.exp(s - m) for s in ss]
        l = functools.reduce(jnp.add, [jnp.sum(e, axis=-1, keepdims=True) for e in es])
        o = functools.reduce(jnp.add, [_dot((e / l).astype(BF16), v[:, cs]) for e, (_, v, _) in zip(es, segs)])
        outs.append(o)
        lses.append(jnp.broadcast_to(m + jnp.log(l), (q.shape[0], SW_DH)))
    return jnp.concatenate(outs, axis=-1), jnp.concatenate(lses, axis=-1)


def _dilp_body(q_ref, kc_ref, kp_ref, vc_ref, vp_ref, o_ref, l_ref, *, dil):
    n = pl.program_id(1)
    qi = lax.broadcasted_iota(jnp.int32, (SW_BLOCK, SW_BLOCK), 0)
    ki = lax.broadcasted_iota(jnp.int32, (SW_BLOCK, SW_BLOCK), 1)
    mprev = (ki >= qi) & (n > 0)
    mcur = ki <= qi

    def residue(r, carry):
        rows = pl.ds(r, SW_BLOCK, stride=dil) if dil > 1 else slice(None)
        q = (q_ref[0, rows, :] * (SW_DH ** -0.5)).astype(BF16)
        o, l = _softmax_heads(q, [(kp_ref[0, rows, :].astype(BF16), vp_ref[0, rows, :].astype(BF16), mprev),
                                  (kc_ref[0, rows, :].astype(BF16), vc_ref[0, rows, :].astype(BF16), mcur)])
        o_ref[0, rows, :] = o
        l_ref[0, rows, :] = l
        return carry

    if dil > 1:
        lax.fori_loop(0, dil // 2, lambda i, c: residue(2 * i + 1, residue(2 * i, c)), 0)
    else:
        residue(0, 0)


def _dilated_prompt(sw, gi, dil, batch, t):
    sb = SW_BLOCK * dil
    sw3 = sw.reshape(batch, t, SW_ALL)
    width = SW_W if dil == 1 else LANES
    nhalf = SW_W // width
    cur = lambda part: (lambda b, n, hf: (b, n, nhalf * (3 * gi + part) + hf))
    prev = lambda part: (lambda b, n, hf: (b, jnp.maximum(n - 1, 0), nhalf * (3 * gi + part) + hf))
    blk = (1, sb, width)
    outm = lambda b, n, hf: (b, n, hf)
    o, l = pl.pallas_call(
        functools.partial(_dilp_body, dil=dil),
        grid=(batch, t // sb, nhalf),
        in_specs=[pl.BlockSpec(blk, cur(0)), pl.BlockSpec(blk, cur(1)), pl.BlockSpec(blk, prev(1)),
                  pl.BlockSpec(blk, cur(2)), pl.BlockSpec(blk, prev(2))],
        out_specs=[pl.BlockSpec(blk, outm), pl.BlockSpec(blk, outm)],
        out_shape=[jax.ShapeDtypeStruct((batch, t, SW_W), F32)] * 2,
        compiler_params=_cparams("arbitrary", "arbitrary", "arbitrary"),
        name=f"dilated_prompt_{gi}",
    )(sw3, sw3, sw3, sw3, sw3)
    return o.reshape(batch * t, SW_W), l.reshape(batch * t, SW_W)


def _dils_body(q_ref, kn_ref, vn_ref, buf_ref, o_ref, l_ref, *, win, dil, t_new, wb):
    tq = q_ref.shape[1]
    q = (q_ref[0] * (SW_DH ** -0.5)).astype(BF16)
    kb = buf_ref[0, :, 0:SW_W].astype(BF16)
    vb = buf_ref[0, :, SW_W:2 * SW_W].astype(BF16)

    def stride_mask(nkeys, first_pos):
        ti = lax.broadcasted_iota(jnp.int32, (tq, nkeys), 0)
        ji = first_pos + lax.broadcasted_iota(jnp.int32, (tq, nkeys), 1)
        d = wb + ti - ji
        return (d >= 0) & (d <= win) & ((d & (dil - 1)) == 0) & (ji < wb + t_new)

    o, l = _softmax_heads(q, [(kb, vb, stride_mask(wb, 0)),
                              (kn_ref[0].astype(BF16), vn_ref[0].astype(BF16), stride_mask(tq, wb))])
    o_ref[0] = o
    l_ref[0] = l


def _dilated_sample(q, kn, vn, buf, win, dil, t_new):
    b, tq, _ = q.shape
    wb = buf.shape[1]
    per = lambda i: (i, 0, 0)
    return pl.pallas_call(
        functools.partial(_dils_body, win=win, dil=dil, t_new=t_new, wb=wb),
        grid=(b,),
        in_specs=[pl.BlockSpec((1, tq, SW_W), per)] * 3 + [pl.BlockSpec((1, wb, 2 * SW_W), per)],
        out_specs=[pl.BlockSpec((1, tq, SW_W), per)] * 2,
        out_shape=[jax.ShapeDtypeStruct((b, tq, SW_W), F32)] * 2,
        compiler_params=_cparams("arbitrary"),
        name=f"dilated_sample_{win}",
    )(q, kn, vn, buf)


def _outproj_body(x_ref, odn_ref, o1, o2, o3, l1, l2, l3, wout_ref, g_ref, wmq_ref, h_ref, qm_ref):
    la, lb, lc = l1[...], l2[...], l3[...]
    m = jnp.maximum(jnp.maximum(la, lb), lc)
    ea, eb, ec = jnp.exp(la - m), jnp.exp(lb - m), jnp.exp(lc - m)
    osw = (ea * o1[...] + eb * o2[...] + ec * o3[...]) / (ea + eb + ec)
    mix = _dot(odn_ref[...].astype(BF16), wout_ref[0:DN_V, :]) + _dot(osw.astype(BF16), wout_ref[DN_V:, :])
    h = x_ref[...] + mix
    h_ref[...] = h
    c = _rms(h, g_ref[...]).astype(BF16)
    qm_ref[...] = (_dot(c, wmq_ref[...]) * (MEM_DH ** -0.5)).astype(BF16)


def _out_proj(x, odn, os_, ls_, wout, g_memq, wmq, tm):
    n = x.shape[0]
    row = lambda i: (i, 0)
    fix = lambda i: (0, 0)
    sws = pl.BlockSpec((tm, SW_W), row)
    return pl.pallas_call(
        _outproj_body,
        grid=(n // tm,),
        in_specs=[pl.BlockSpec((tm, D_MODEL), row), pl.BlockSpec((tm, DN_V), row)] + [sws] * 6 +
                 [pl.BlockSpec((DN_V + SW_W, D_MODEL), fix), pl.BlockSpec((1, D_MODEL), fix),
                  pl.BlockSpec((D_MODEL, D_MODEL), fix)],
        out_specs=[pl.BlockSpec((tm, D_MODEL), row), pl.BlockSpec((tm, D_MODEL), row)],
        out_shape=[jax.ShapeDtypeStruct((n, D_MODEL), F32), jax.ShapeDtypeStruct((n, D_MODEL), BF16)],
        compiler_params=_cparams("arbitrary"),
        name="out_proj",
    )(x, odn, *os_, *ls_, wout, g_memq, wmq)


def _memkv_body(m_ref, g_ref, w_ref, kv_ref):
    kv_ref[...] = _dot(_rms(m_ref[...], g_ref[...]).astype(BF16), w_ref[...])


def _memory_kv(mem, g, w, tm=256, tn=1024):
    n = mem.shape[0]
    ncol = w.shape[1]
    return pl.pallas_call(
        _memkv_body,
        grid=(n // tm, ncol // tn),
        in_specs=[pl.BlockSpec((tm, D_MODEL), lambda i, j: (i, 0)), pl.BlockSpec((1, D_MODEL), lambda i, j: (0, 0)),
                  pl.BlockSpec((D_MODEL, tn), lambda i, j: (0, j))],
        out_specs=pl.BlockSpec((tm, tn), lambda i, j: (i, j)),
        out_shape=jax.ShapeDtypeStruct((n, ncol), F32),
        compiler_params=_cparams("arbitrary", "arbitrary"),
        name="memory_kv",
    )(mem, g, w)


def _memattn_body(h_ref, q_ref, k_ref, v_ref, wmo_ref, o_ref):
    q = q_ref[0]
    outs = []
    for hd in range(MEM_HEADS):
        cs = slice(hd * MEM_DH, (hd + 1) * MEM_DH)
        s = _dot_nt(q[:, cs], k_ref[0, :, cs])
        m = jnp.max(s, axis=-1, keepdims=True)
        e = jnp.exp(s - m)
        p = e / jnp.sum(e, axis=-1, keepdims=True)
        outs.append(_dot(p.astype(BF16), v_ref[0, :, cs]))
    o = jnp.concatenate(outs, axis=-1).astype(BF16)
    o_ref[0] = h_ref[0] + _dot(o, wmo_ref[...])


def _mem_attend(h, qm, kvb, wmo, tm):
    b, t, _ = h.shape
    blk = lambda i, j: (i, j, 0)
    return pl.pallas_call(
        _memattn_body,
        grid=(b, t // tm),
        in_specs=[pl.BlockSpec((1, tm, D_MODEL), blk), pl.BlockSpec((1, tm, D_MODEL), blk),
                  pl.BlockSpec((1, MEM_LEN, D_MODEL), lambda i, j: (i, 0, 0)),
                  pl.BlockSpec((1, MEM_LEN, D_MODEL), lambda i, j: (i, 0, 1)),
                  pl.BlockSpec((D_MODEL, D_MODEL), lambda i, j: (0, 0))],
        out_specs=pl.BlockSpec((1, tm, D_MODEL), blk),
        out_shape=jax.ShapeDtypeStruct((b, t, D_MODEL), F32),
        compiler_params=_cparams("arbitrary", "arbitrary"),
        name="mem_attend",
    )(h, qm, kvb, kvb, wmo)


def _topk_rows(s, k, payload=None):
    r = s.shape[0]
    rid = lax.broadcasted_iota(jnp.int32, s.shape, 0).astype(F32)
    vals, picks = [], []
    for _ in range(k):
        m = jnp.max(s, axis=0, keepdims=True)
        idx = jnp.min(jnp.where(s == m, rid, float(r)), axis=0, keepdims=True)
        sel = rid == idx
        vals.append(m)
        if payload is None:
            picks.append(idx)
        else:
            picks.append(jnp.sum(jnp.where(sel, payload, 0.0), axis=0, keepdims=True))
        s = jnp.where(sel, -jnp.inf, s)
    return vals, picks


def _pair_candidates(v1, v2, combine, fill):
    k = PEER_TOPK
    v2all = jnp.concatenate(v2, axis=0)
    bid = lax.broadcasted_iota(jnp.int32, v2all.shape, 0)
    parts = []
    for a in range(4):
        parts.append(jnp.where((a + 1) * (bid + 1) <= k, combine(v1[a], v2all), fill))
    for a in range(4, 8):
        half = combine(v1[a], v2all[0:SUBLANES])
        parts.append(jnp.where((a + 1) * (bid[0:SUBLANES] + 1) <= k, half, fill))
    parts.append(combine(jnp.concatenate(v1[8:16], axis=0), v2[0]))
    return jnp.concatenate(parts, axis=0)


def _route_body(h_ref, g_ref, wh_ref, wl_ref, kh_ref, kl_ref, fhl_ref, eid_ref, gate_ref):
    f = _rms(h_ref[...], g_ref[...])
    fh, fl = _split(f)
    fhl_ref[...] = jnp.concatenate([fh, fl], axis=-1)
    qt = _dot_nt(wh_ref[...], fh) + _dot_nt(wh_ref[...], fl) + _dot_nt(wl_ref[...], fh)
    eids, gates = [], []
    for hd in range(PEER_HEADS):
        sv, si = [], []
        for p in range(2):
            hp = 2 * hd + p
            qh, ql = _split(qt[hp * PEER_DHALF:(hp + 1) * PEER_DHALF, :])
            s = _dot(kh_ref[hp], qh) + _dot(kh_ref[hp], ql) + _dot(kl_ref[hp], qh)
            v, i = _topk_rows(s, PEER_TOPK)
            sv.append(v)
            si.append(i)
        cand = _pair_candidates(sv[0], sv[1], lambda x, y: x + y, -jnp.inf)
        cid = _pair_candidates(si[0], si[1], lambda x, y: x * PEER_NKEYS + y, 0.0)
        cv, ce = _topk_rows(cand, PEER_TOPK, payload=cid)
        cv = jnp.concatenate(cv, axis=0)
        ex = jnp.exp(cv - cv[0:1, :])
        gates.append(ex / jnp.sum(ex, axis=0, keepdims=True))
        eids.append(jnp.concatenate(ce, axis=0))
    eid_ref[...] = jnp.concatenate(eids, axis=0).T.astype(jnp.int32)
    gate_ref[...] = jnp.concatenate(gates, axis=0).T


def _peer_route(h, g, wpq_h, wpq_l, keys_h, keys_l):
    n = h.shape[0]
    tm = LANES
    row = lambda i: (i, 0)
    fix2 = lambda i: (0, 0)
    fix3 = lambda i: (0, 0, 0)
    nq = PEER_HEADS * PEER_DQ
    return pl.pallas_call(
        _route_body,
        grid=(n // tm,),
        in_specs=[pl.BlockSpec((tm, D_MODEL), row), pl.BlockSpec((1, D_MODEL), fix2),
                  pl.BlockSpec((nq, D_MODEL), fix2), pl.BlockSpec((nq, D_MODEL), fix2),
                  pl.BlockSpec((2 * PEER_HEADS, PEER_NKEYS, PEER_DHALF), fix3),
                  pl.BlockSpec((2 * PEER_HEADS, PEER_NKEYS, PEER_DHALF), fix3)],
        out_specs=[pl.BlockSpec((tm, 2 * D_MODEL), row), pl.BlockSpec((tm, PEER_PAIRS), row),
                   pl.BlockSpec((tm, PEER_PAIRS), row)],
        out_shape=[jax.ShapeDtypeStruct((n, 2 * D_MODEL), BF16), jax.ShapeDtypeStruct((n, PEER_PAIRS), jnp.int32),
                   jax.ShapeDtypeStruct((n, PEER_PAIRS), F32)],
        compiler_params=_cparams("arbitrary"),
        name="peer_route",
    )(h, g, wpq_h, wpq_l, keys_h, keys_l)


def _load_table(tbl_hbm, tbl, sem):
    @pl.when(pl.program_id(0) == 0)
    def _():
        cp = pltpu.make_async_copy(tbl_hbm, tbl, sem)
        cp.start()
        cp.wait()


def _gather_rows(tbl, eid_ref, t, xbuf):
    for k in range(PEER_PAIRS):
        xbuf[k * SUBLANES:(k + 1) * SUBLANES, :] = tbl[eid_ref[t, k]]


def _skewed_token_loop(te, xbufs, gather, compute, unroll=16):
    gather(0, xbufs[0])

    def body(i, carry):
        t = unroll * i
        for j in range(unroll):
            gather(jnp.minimum(t + j + 1, te - 1), xbufs[(j + 1) % 2])
            compute(t + j, xbufs[j % 2])
        return carry

    lax.fori_loop(0, te // unroll, body, 0)


def _chunk_eye():
    r = lax.broadcasted_iota(jnp.int32, (SUBLANES, PEER_PAIRS * SUBLANES), 0)
    c = lax.broadcasted_iota(jnp.int32, (SUBLANES, PEER_PAIRS * SUBLANES), 1)
    return r == (c & (SUBLANES - 1))


def _peer_hid_body(eid_ref, fhl_ref, gate_ref, tbl_hbm, w_ref, tbl, xbuf0, xbuf1, acc, sem, *, te):
    _load_table(tbl_hbm, tbl, sem)
    eye = _chunk_eye()

    def compute(t, xbuf):
        res = _dot_nt(fhl_ref[t], xbuf[...])
        r = res[0:SUBLANES] + res[SUBLANES:2 * SUBLANES]
        acc[pl.ds(t, 1), :] = jnp.sum(jnp.where(eye, r, 0.0), axis=0, keepdims=True)

    _skewed_token_loop(te, (xbuf0, xbuf1), lambda t, xb: _gather_rows(tbl, eid_ref, t, xb), compute)
    a = acc[...]
    a1 = a.astype(BF16)
    a2 = (a - a1.astype(F32)).astype(BF16)
    a3 = (a - a1.astype(F32) - a2.astype(F32)).astype(BF16)
    fr = lax.broadcasted_iota(jnp.int32, (PEER_PAIRS * SUBLANES, PEER_PAIRS), 0) // SUBLANES
    fc = lax.broadcasted_iota(jnp.int32, (PEER_PAIRS * SUBLANES, PEER_PAIRS), 1)
    fold = (fr == fc).astype(BF16)
    hid = _dot(a1, fold) + _dot(a2, fold) + _dot(a3, fold)
    c = math.sqrt(2.0 / math.pi)
    gelu = 0.5 * hid * (1.0 + jnp.tanh(c * (hid + 0.044715 * (hid * hid * hid))))
    w_ref[...] = gate_ref[...] * gelu


def _peer_hid(eid, fhl, gate, tbl, te):
    n = eid.shape[0]
    row = lambda i: (i, 0)
    return pl.pallas_call(
        functools.partial(_peer_hid_body, te=te),
        grid=(n // te,),
        in_specs=[pl.BlockSpec((te, PEER_PAIRS), row, memory_space=pltpu.SMEM),
                  pl.BlockSpec((te, 2 * SUBLANES, LANES), lambda i: (i, 0, 0)),
                  pl.BlockSpec((te, PEER_PAIRS), row),
                  pl.BlockSpec(memory_space=pl.ANY)],
        out_specs=pl.BlockSpec((te, PEER_PAIRS), row),
        out_shape=jax.ShapeDtypeStruct((n, PEER_PAIRS), F32),
        scratch_shapes=[pltpu.VMEM((PEER_N, SUBLANES, LANES), BF16),
                        pltpu.VMEM((PEER_PAIRS * SUBLANES, LANES), BF16),
                        pltpu.VMEM((PEER_PAIRS * SUBLANES, LANES), BF16),
                        pltpu.VMEM((te, PEER_PAIRS * SUBLANES), F32), pltpu.SemaphoreType.DMA],
        compiler_params=_cparams("arbitrary"),
        name="peer_hid",
    )(eid, fhl, gate, tbl)


def _peer_out_body(eid_ref, w_ref, h_ref, g_ref, tbl_hbm, o_ref, tbl, xbuf0, xbuf1, wexp_hi, wexp_lo, sem, *, te):
    _load_table(tbl_hbm, tbl, sem)
    eye = _chunk_eye()
    w = w_ref[...]
    wh = w.astype(BF16)
    wl = (w - wh.astype(F32)).astype(BF16)
    er = lax.broadcasted_iota(jnp.int32, (PEER_PAIRS, PEER_PAIRS * SUBLANES), 0)
    ec = lax.broadcasted_iota(jnp.int32, (PEER_PAIRS, PEER_PAIRS * SUBLANES), 1) // SUBLANES
    expand = (er == ec).astype(BF16)
    wexp_hi[...] = _dot(wh, expand)
    wexp_lo[...] = _dot(wl, expand)
    wide = (SUBLANES, PEER_PAIRS * SUBLANES)

    def compute(t, xbuf):
        hi = jnp.where(eye, jnp.broadcast_to(wexp_hi[pl.ds(t, 1), :], wide), 0.0)
        lo = jnp.where(eye, jnp.broadcast_to(wexp_lo[pl.ds(t, 1), :], wide), 0.0)
        res = _dot(jnp.concatenate([hi, lo], axis=0).astype(BF16), xbuf[...])
        o_ref[t] = h_ref[t] + (res[0:SUBLANES] + res[SUBLANES:2 * SUBLANES])

    _skewed_token_loop(te, (xbuf0, xbuf1), lambda t, xb: _gather_rows(tbl, eid_ref, t, xb), compute)
    h = o_ref[...]
    ms = jnp.sum(jnp.sum(h * h, axis=2, keepdims=True), axis=1, keepdims=True) * (1.0 / D_MODEL)
    o_ref[...] = h * lax.rsqrt(ms + NORM_EPS) * g_ref[...]


def _peer_out(eid, w, h3, g3, tbl, te):
    n = eid.shape[0]
    row = lambda i: (i, 0)
    row3 = lambda i: (i, 0, 0)
    return pl.pallas_call(
        functools.partial(_peer_out_body, te=te),
        grid=(n // te,),
        in_specs=[pl.BlockSpec((te, PEER_PAIRS), row, memory_space=pltpu.SMEM),
                  pl.BlockSpec((te, PEER_PAIRS), row),
                  pl.BlockSpec((te, SUBLANES, LANES), row3),
                  pl.BlockSpec((1, SUBLANES, LANES), lambda i: (0, 0, 0)),
                  pl.BlockSpec(memory_space=pl.ANY)],
        out_specs=pl.BlockSpec((te, SUBLANES, LANES), row3),
        out_shape=jax.ShapeDtypeStruct((n, SUBLANES, LANES), F32),
        scratch_shapes=[pltpu.VMEM((PEER_N, SUBLANES, LANES), BF16),
                        pltpu.VMEM((PEER_PAIRS * SUBLANES, LANES), BF16),
                        pltpu.VMEM((PEER_PAIRS * SUBLANES, LANES), BF16),
                        pltpu.VMEM((te, PEER_PAIRS * SUBLANES), F32),
                        pltpu.VMEM((te, PEER_PAIRS * SUBLANES), F32), pltpu.SemaphoreType.DMA],
        compiler_params=_cparams("arbitrary"),
        name="peer_out",
    )(eid, w, h3, g3, tbl)


def _pack_table(t):
    return t.astype(BF16).reshape(t.shape[0], SUBLANES, LANES)


def _prep_weights(g_mix, w_in, conv_w, a_log, dt_bias, g_onorm, w_out, g_memq, g_memkv, w_mq, w_mkv, w_mo,
                  g_ffn, w_pq, sub_keys, expert_u, expert_v, g_final):
    c0 = 2 * DN_QK + DN_V
    gate_cols = w_in[:, c0:c0 + 2 * DN_HEADS]
    z_cols = w_in[:, c0 + 2 * DN_HEADS:c0 + 2 * DN_HEADS + DN_V]
    sw_cols = w_in[:, c0 + 2 * DN_HEADS + DN_V:]
    w_main = jnp.concatenate([w_in[:, :c0], z_cols, sw_cols], axis=1).astype(BF16)
    wg = jnp.pad(gate_cols, ((0, 0), (0, LANES - 2 * DN_HEADS)))
    wgh, wgl = _split(wg)
    wpq_t = w_pq.T
    wpq_h, wpq_l = _split(wpq_t)
    keys = sub_keys.reshape(2 * PEER_HEADS, PEER_NKEYS, PEER_DHALF)
    keys_h, keys_l = _split(keys)
    pad_row = lambda v: jnp.pad(v.astype(F32), (0, LANES - v.shape[0])).reshape(1, LANES)
    return dict(
        g_mix=g_mix.reshape(1, D_MODEL), w_main=w_main, wgh=wgh, wgl=wgl, conv_w=conv_w,
        alog_row=pad_row(a_log), dtb_row=pad_row(dt_bias), g_onorm=g_onorm.reshape(1, DN_DV),
        w_out=w_out.astype(BF16), g_memq=g_memq.reshape(1, D_MODEL), g_memkv=g_memkv.reshape(1, D_MODEL),
        w_mq=w_mq.astype(BF16), w_mkv=w_mkv.astype(BF16), w_mo=w_mo.astype(BF16),
        g_ffn=g_ffn.reshape(1, D_MODEL), wpq_h=wpq_h, wpq_l=wpq_l, keys_h=keys_h, keys_l=keys_l,
        tbl_u=_pack_table(expert_u), tbl_v=_pack_table(expert_v), g_final=g_final.reshape(1, D_MODEL))


def _layer(x, cbuf8, s0, wins, kvb, w, *, chunk, nchunk, t_valid, t_new):
    b, t, _ = x.shape
    n = b * t
    tm = min(256, n)
    x2 = x.reshape(n, D_MODEL)
    qkv, z, sw, gates = _in_proj(x2, w["g_mix"], w["w_main"], w["wgh"], w["wgl"], tm)
    odn, s_new = _deltanet(qkv.reshape(b, t, CONV_DIM), z.reshape(b, t, DN_V), gates.reshape(b, t, LANES),
                           cbuf8, s0, w["conv_w"], w["alog_row"], w["dtb_row"], w["g_onorm"],
                           chunk=chunk, nchunk=nchunk, t_valid=t_valid)
    outs, lses = [], []
    for gi, (win, dil) in enumerate(SW_GROUPS):
        if wins is None:
            o, l = _dilated_prompt(sw, gi, dil, b, t)
        else:
            part = lambda p: sw[:, (3 * gi + p) * SW_W:(3 * gi + p + 1) * SW_W].reshape(b, t, SW_W)
            o, l = _dilated_sample(part(0), part(1), part(2), wins[gi], win, dil, t_new)
            o, l = o.reshape(n, SW_W), l.reshape(n, SW_W)
        outs.append(o)
        lses.append(l)
    h1, qm = _out_proj(x2, odn.reshape(n, DN_V), outs, lses, w["w_out"], w["g_memq"], w["w_mq"], tm)
    tma = min(512, t)
    h2 = _mem_attend(h1.reshape(b, t, D_MODEL), qm.reshape(b, t, D_MODEL), kvb, w["w_mo"], tma)
    h2 = h2.reshape(n, D_MODEL)
    fhl, eid, gate = _peer_route(h2, w["g_ffn"], w["wpq_h"], w["wpq_l"], w["keys_h"], w["keys_l"])
    te = min(128, n)
    wgt = _peer_hid(eid, fhl.reshape(n, 2 * SUBLANES, LANES), gate, w["tbl_u"], te)
    y = _peer_out(eid, wgt, h2.reshape(n, SUBLANES, LANES), w["g_final"].reshape(1, SUBLANES, LANES), w["tbl_v"], te)
    return y.reshape(b, t, D_MODEL), qkv.reshape(b, t, CONV_DIM), s_new, sw.reshape(b, t, SW_ALL)


def kernel(x_prompt, x_sample, state_delta, state_conv, cache_win1, cache_win2, cache_win3, cache_mem_kv,
           mem_prompt, g_mix, w_in, conv_w, a_log, dt_bias, g_onorm, w_out, g_memq, g_memkv, w_mq, w_mkv,
           w_mo, g_ffn, w_pq, sub_keys, expert_u, expert_v, g_final):
    w = _prep_weights(g_mix[0], w_in[0], conv_w[0], a_log[0], dt_bias[0], g_onorm[0], w_out[0], g_memq[0],
                      g_memkv[0], w_mq[0], w_mkv[0], w_mo[0], g_ffn[0], w_pq[0], sub_keys[0], expert_u[0],
                      expert_v[0], g_final)
    bp, tp, _ = x_prompt.shape
    bs, ts, _ = x_sample.shape

    kv_p = _memory_kv(mem_prompt.reshape(bp * MEM_LEN, D_MODEL), w["g_memkv"], w["w_mkv"])
    kv_p3 = kv_p.reshape(bp, MEM_LEN, 2 * D_MODEL)
    y_p, qkv_p, sdelta_p, sw_p = _layer(
        x_prompt, jnp.zeros((bp, SUBLANES, CONV_DIM), F32), jnp.zeros((bp, DN_HEADS, DN_DK, DN_DV), F32),
        None, kv_p3.astype(BF16), w, chunk=64, nchunk=2, t_valid=None, t_new=None)
    p_conv = qkv_p[:, tp - (CONV_W - 1):]
    p_wins = []
    for gi, (win, _) in enumerate(SW_GROUPS):
        keep = min(win, tp)
        kvrows = sw_p[:, tp - keep:].reshape(bp, keep, len(SW_GROUPS), 3, SW_HEADS, SW_DH)
        p_wins.append(kvrows[:, :, gi, 1:3])
    p_mem = kv_p.reshape(bp, MEM_LEN, 2, MEM_HEADS, MEM_DH)

    tpad = SUBLANES
    xs = jnp.pad(x_sample, ((0, 0), (0, tpad - ts), (0, 0)))
    cbuf8 = jnp.pad(state_conv[0], ((0, 0), (SUBLANES - (CONV_W - 1), 0), (0, 0)))
    wins = [c[0].reshape(bs, c.shape[2], 2 * SW_W) for c in (cache_win1, cache_win2, cache_win3)]
    kv_s = cache_mem_kv[0].reshape(bs, MEM_LEN, 2 * D_MODEL).astype(BF16)
    y_s, qkv_s, sdelta_s, sw_s = _layer(xs, cbuf8, state_delta[0], wins, kv_s, w,
                                        chunk=SUBLANES, nchunk=1, t_valid=ts, t_new=ts)
    y_s = y_s[:, :ts]
    s_conv = jnp.concatenate([state_conv[0], qkv_s[:, :ts]], axis=1)[:, -(CONV_W - 1):]
    s_wins = []
    for gi, (win, _) in enumerate(SW_GROUPS):
        new = sw_s[:, :ts].reshape(bs, ts, len(SW_GROUPS), 3, SW_HEADS, SW_DH)[:, :, gi, 1:3]
        cache = (cache_win1, cache_win2, cache_win3)[gi][0]
        full = jnp.concatenate([cache, new], axis=1)
        s_wins.append(full[:, -min(win, full.shape[1]):])

    st = lambda a: a[None]
    return (y_p, y_s, st(sdelta_p), st(p_conv), st(p_wins[0]), st(p_wins[1]), st(p_wins[2]), st(p_mem),
            st(sdelta_s), st(s_conv), st(s_wins[0]), st(s_wins[1]), st(s_wins[2]))
```
